```python
import jax
import jax.numpy as jnp
from jax import lax
import numpy as np

D_MODEL = 2048
BATCH = 2
SEQ = 4096
DEPTH = 4
DEC_BATCH = 128
DEC_SEQ = 1
PAST_LEN = 8192
PAGE_SIZE = 128

D_MIX = D_MODEL
C_CONV = D_MIX // 4
CONV_WIDTH = 31
MLA_HEADS = 6
MLA_NOPE = 128
MLA_ROPE = 64
MLA_V = 128
MLA_Q_RANK = 512
MLA_KV_RANK = 256
FOX_HEADS = 6
FOX_HEAD_DIM = 128
FOX_GATE_BIAS = 2.0
ROPE_THETA = 10000.0
N_MEM = 256
XA_HEADS = 4
XA_HEAD_DIM = 128
N_GROUPS = 4
EXPERTS_PER_GROUP = 4
N_EXPERTS = N_GROUPS * EXPERTS_PER_GROUP
TOP_K = 2
D_EXPERT = 512

Q_BLOCK = 128
EPS = 1e-6
MLA_SCALE = (MLA_NOPE + MLA_ROPE) ** -0.5
FOX_SCALE = FOX_HEAD_DIM ** -0.5
XA_SCALE = XA_HEAD_DIM ** -0.5
IN_SIZES = (C_CONV, C_CONV,
            MLA_Q_RANK, MLA_KV_RANK, MLA_ROPE,
            FOX_HEADS * FOX_HEAD_DIM, FOX_HEAD_DIM, FOX_HEAD_DIM, FOX_HEADS)
D_IN = sum(IN_SIZES)

kernel_name = 'hymba_conv_mla_fox_hmoe_step'

F32 = jnp.float32


def rms_norm(x, g):
    xf = x.astype(F32)
    y = xf * lax.rsqrt(jnp.mean(xf * xf, axis=-1, keepdims=True) + EPS)
    return (y * g.astype(F32)).astype(x.dtype)


def layer_norm(x, g, b):
    xf = x.astype(F32)
    mu = jnp.mean(xf, axis=-1, keepdims=True)
    var = jnp.mean(jnp.square(xf - mu), axis=-1, keepdims=True)
    return ((xf - mu) * lax.rsqrt(var + EPS) * g.astype(F32) + b.astype(F32)).astype(x.dtype)


def rope(x, pos):
    half = x.shape[-1] // 2
    inv = ROPE_THETA ** (-jnp.arange(half, dtype=F32) / half)
    ang = pos.astype(F32)[:, None] * inv[None, :]
    shp = (pos.shape[0],) + (1,) * (x.ndim - 3) + (half,)
    cos, sin = jnp.cos(ang).reshape(shp), jnp.sin(ang).reshape(shp)
    xf = x.astype(F32)
    x1, x2 = xf[..., :half], xf[..., half:]
    return jnp.concatenate([x1 * cos - x2 * sin, x1 * sin + x2 * cos], axis=-1).astype(x.dtype)


def gather_pages(pool, page_table):
    g = pool[page_table]
    return g.reshape((page_table.shape[0], -1) + pool.shape[2:])


def blocked_causal_attention(score_fn, v):
    b, s_len, dv = v.shape
    kpos = jnp.arange(s_len)

    def one_block(i):
        s = score_fn(i)
        qpos = i * Q_BLOCK + jnp.arange(Q_BLOCK)
        s = jnp.where(kpos[None, :] <= qpos[:, None], s, -jnp.inf)
        p = jax.nn.softmax(s, axis=-1)
        return jnp.einsum('bhqk,bkv->bqhv', p, v)

    o = lax.map(one_block, jnp.arange(s_len // Q_BLOCK))
    return jnp.moveaxis(o, 0, 1).reshape(b, s_len, o.shape[3], dv)


def decode_attention(s_past, s_new, v_past, v_new):
    t = s_new.shape[-1]
    causal = jnp.arange(t)[None, :] <= jnp.arange(t)[:, None]
    s = jnp.concatenate([s_past, jnp.where(causal, s_new, -jnp.inf)], axis=-1)
    p = jax.nn.softmax(s, axis=-1)
    n_past = s_past.shape[-1]
    return (jnp.einsum('nhtp,npv->nthv', p[..., :n_past], v_past)
            + jnp.einsum('nhtq,nqv->nthv', p[..., n_past:], v_new))


def conv_module(u_ext, conv_w, conv_b, ln_g, ln_b):
    y = lax.conv_general_dilated(u_ext, conv_w[:, None, :].astype(u_ext.dtype), window_strides=(1,),
                                 padding='VALID', dimension_numbers=('NWC', 'WIO', 'NWC'),
                                 feature_group_count=C_CONV) + conv_b
    return jax.nn.silu(layer_norm(y, ln_g, ln_b))


def project_mixers(xn, pos, w_in, q_norm, w_uq, kv_norm, w_uk, f_bias):
    n, t, _ = xn.shape
    h = jnp.einsum('ntd,de->nte', xn, w_in)
    bounds, acc = [], 0
    for size in IN_SIZES[:-1]:
        acc += size
        bounds.append(acc)
    a, g, cq, ckv, kr, fq, fk, fv, ff = jnp.split(h, bounds, axis=-1)
    u = a * jax.nn.sigmoid(g)
    q = jnp.einsum('ntr,rhe->nthe', rms_norm(cq, q_norm), w_uq)
    q_lat = jnp.einsum('nthe,rhe->nthr', q[..., :MLA_NOPE], w_uk).astype(F32)
    q_rope = rope(q[..., MLA_NOPE:], pos).astype(F32)
    ckv = rms_norm(ckv, kv_norm)
    kr = rope(kr, pos)
    fq = fq.reshape(n, t, FOX_HEADS, FOX_HEAD_DIM).astype(F32)
    logf = jax.nn.log_sigmoid(ff.astype(F32) + f_bias.astype(F32))
    return u, q_lat, q_rope, ckv, kr, fq, fk, fv, logf


def merge_heads(conv_o, mla_lat, fox_o, w_uv, w_out, dtype):
    n, t = conv_o.shape[:2]
    mla_o = jnp.einsum('nthr,rhv->nthv', mla_lat, w_uv.astype(F32)).reshape(n, t, -1)
    cat = jnp.concatenate([conv_o.astype(dtype), mla_o.astype(dtype),
                           fox_o.reshape(n, t, -1).astype(dtype)], axis=-1)
    return jnp.einsum('nte,ed->ntd', cat, w_out)


def mixer_prompt(xn, pos, mw):
    w_in, conv_w, conv_b, ln_g, ln_b, q_norm, w_uq, kv_norm, w_uk, w_uv, f_bias, w_out = mw
    u, q_lat, q_rope, ckv, kr, fq, fk, fv, logf = project_mixers(xn, pos, w_in, q_norm, w_uq, kv_norm, w_uk, f_bias)
    b = xn.shape[0]
    u_ext = jnp.concatenate([jnp.zeros((b, CONV_WIDTH - 1, C_CONV), u.dtype), u], axis=1)
    conv_o = conv_module(u_ext, conv_w, conv_b, ln_g, ln_b)
    ckv32, kr32 = ckv.astype(F32), kr.astype(F32)

    def mla_scores(i):
        ql = lax.dynamic_slice_in_dim(q_lat, i * Q_BLOCK, Q_BLOCK, axis=1)
        qr = lax.dynamic_slice_in_dim(q_rope, i * Q_BLOCK, Q_BLOCK, axis=1)
        return (jnp.einsum('bqhr,bkr->bhqk', ql, ckv32) + jnp.einsum('bqhe,bke->bhqk', qr, kr32)) * MLA_SCALE

    mla_lat = blocked_causal_attention(mla_scores, ckv32)
    cum = jnp.transpose(jnp.cumsum(logf, axis=1), (0, 2, 1))
    fk32, fv32 = fk.astype(F32), fv.astype(F32)

    def fox_scores(i):
        qi = lax.dynamic_slice_in_dim(fq, i * Q_BLOCK, Q_BLOCK, axis=1)
        ci = lax.dynamic_slice_in_dim(cum, i * Q_BLOCK, Q_BLOCK, axis=2)
        return (jnp.einsum('bqhd,bkd->bhqk', qi, fk32) * FOX_SCALE
                + ci[:, :, :, None] - cum[:, :, None, :])

    fox_o = blocked_causal_attention(fox_scores, fv32)
    y = merge_heads(conv_o, mla_lat, fox_o, w_uv, w_out, xn.dtype)
    return y, (ckv, kr, fk, fv, logf, u_ext[:, -(CONV_WIDTH - 1):])


def mixer_sample(xn, pos, mw, conv_state, ckv_pool, kr_pool, fk_pool, fv_pool, logf_pool, page_table):
    w_in, conv_w, conv_b, ln_g, ln_b, q_norm, w_uq, kv_norm, w_uk, w_uv, f_bias, w_out = mw
    u, q_lat, q_rope, ckv, kr, fq, fk, fv, logf = project_mixers(xn, pos, w_in, q_norm, w_uq, kv_norm, w_uk, f_bias)
    u_ext = jnp.concatenate([conv_state.astype(u.dtype), u], axis=1)
    conv_o = conv_module(u_ext, conv_w, conv_b, ln_g, ln_b)
    ckv32, kr32 = ckv.astype(F32), kr.astype(F32)
    ckv_p = gather_pages(ckv_pool, page_table)
    kr_p = gather_pages(kr_pool, page_table)
    s_past = (jnp.einsum('nthr,npr->nhtp', q_lat, ckv_p) + jnp.einsum('nthe,npe->nhtp', q_rope, kr_p)) * MLA_SCALE
    s_new = (jnp.einsum('nthr,nsr->nhts', q_lat, ckv32) + jnp.einsum('nthe,nse->nhts', q_rope, kr32)) * MLA_SCALE
    mla_lat = decode_attention(s_past, s_new, ckv_p, ckv32)
    fk32, fv32 = fk.astype(F32), fv.astype(F32)
    fk_p = gather_pages(fk_pool, page_table)
    fv_p = gather_pages(fv_pool, page_table)
    cum_p = jnp.cumsum(gather_pages(logf_pool, page_table).astype(F32), axis=1)
    suffix = jnp.transpose(cum_p[:, -1:, :] - cum_p, (0, 2, 1))
    cum_n = jnp.transpose(jnp.cumsum(logf, axis=1), (0, 2, 1))
    s_past = (jnp.einsum('nthd,npd->nhtp', fq, fk_p) * FOX_SCALE
              + suffix[:, :, None, :] + cum_n[:, :, :, None])
    s_new = (jnp.einsum('nthd,nsd->nhts', fq, fk32) * FOX_SCALE
             + cum_n[:, :, :, None] - cum_n[:, :, None, :])
    fox_o = decode_attention(s_past, s_new, fv_p, fv32)
    y = merge_heads(conv_o, mla_lat, fox_o, w_uv, w_out, xn.dtype)
    return y, (ckv, kr, fk, fv, logf, u_ext[:, -(CONV_WIDTH - 1):])


def memory_kv(mem, g, w_kv):
    n, m, _ = mem.shape
    kv = jnp.einsum('nmd,de->nme', rms_norm(mem, g), w_kv).reshape(n, m, 2, XA_HEADS, XA_HEAD_DIM)
    return kv[:, :, 0], kv[:, :, 1]


def cross_attention(xn, mk, mv, w_q, w_o):
    n, t, _ = xn.shape
    q = jnp.einsum('ntd,de->nte', xn, w_q).reshape(n, t, XA_HEADS, XA_HEAD_DIM).astype(F32)
    p = jax.nn.softmax(jnp.einsum('nthd,nmhd->nhtm', q, mk.astype(F32)) * XA_SCALE, axis=-1)
    o = jnp.einsum('nhtm,nmhd->nthd', p, mv.astype(F32)).reshape(n, t, -1).astype(xn.dtype)
    return jnp.einsum('nte,ed->ntd', o, w_o)


def hier_moe(x, wg, bg, we, be, w_gate, w_up, w_down):
    n, t, d = x.shape
    xt = x.reshape(n * t, d)
    g_logits = jnp.einsum('md,dg->mg', xt, wg).astype(F32) + bg.astype(F32)
    g_sel = jnp.argmax(g_logits, axis=-1)
    g_prob = jnp.take_along_axis(jax.nn.softmax(g_logits, axis=-1), g_sel[:, None], axis=1)
    e_logits = (jnp.einsum('md,de->me', xt, we).astype(F32) + be.astype(F32)).reshape(-1, N_GROUPS, EXPERTS_PER_GROUP)
    e_in = jnp.take_along_axis(e_logits, g_sel[:, None, None], axis=1)[:, 0]
    top_v, top_i = lax.top_k(e_in, TOP_K)
    top_w = jax.nn.softmax(top_v, axis=-1) * g_prob
    idx = g_sel[:, None] * EXPERTS_PER_GROUP + top_i
    gates = jnp.sum(jax.nn.one_hot(idx, N_EXPERTS, dtype=F32) * top_w[..., None], axis=1)
    h = jax.nn.silu(jnp.einsum('md,edf->mef', xt, w_gate)) * jnp.einsum('md,edf->mef', xt, w_up)
    y = jnp.einsum('mef,efd->md', h * gates[..., None].astype(h.dtype), w_down)
    return y.reshape(n, t, d)


def setup_inputs(seed: int = 0) -> dict:
    key = jax.random.key(seed)
    ks = iter(jax.random.split(key, 64))
    n_pages = PAST_LEN // PAGE_SIZE
    n_used = DEC_BATCH * n_pages
    n_pool = n_used + (n_used + 3) // 4

    def nrm(shape, scale=1.0):
        return jax.random.normal(next(ks), shape, F32) * scale

    def gain(shape):
        return 1.0 + nrm(shape, 0.01)

    page_table = jax.random.permutation(next(ks), n_pool)[:n_used].reshape(DEC_BATCH, n_pages).astype(jnp.int32)
    return {
        'x_prompt': nrm((BATCH, SEQ, D_MODEL)),
        'x_sample': nrm((DEC_BATCH, DEC_SEQ, D_MODEL)),
        'cache_mla_ckv': nrm((DEPTH, n_pool, PAGE_SIZE, MLA_KV_RANK)),
        'cache_mla_krope': nrm((DEPTH, n_pool, PAGE_SIZE, MLA_ROPE)),
        'cache_fox_k': nrm((DEPTH, n_pool, PAGE_SIZE, FOX_HEAD_DIM)),
        'cache_fox_v': nrm((DEPTH, n_pool, PAGE_SIZE, FOX_HEAD_DIM)),
        'cache_fox_logf': jax.nn.log_sigmoid(nrm((DEPTH, n_pool, PAGE_SIZE, FOX_HEADS)) + FOX_GATE_BIAS),
        'cache_mem_k': nrm((DEPTH, DEC_BATCH, N_MEM, XA_HEADS, XA_HEAD_DIM)),
        'cache_mem_v': nrm((DEPTH, DEC_BATCH, N_MEM, XA_HEADS, XA_HEAD_DIM)),
        'state_conv': nrm((DEPTH, DEC_BATCH, CONV_WIDTH - 1, C_CONV), 0.5),
        'page_table': page_table,
        'mem_prompt': nrm((BATCH, N_MEM, D_MODEL)),
        'norm_mix': gain((DEPTH, D_MODEL)),
        'w_in': nrm((DEPTH, D_MODEL, D_IN), D_MODEL ** -0.5),
        'conv_w': nrm((DEPTH, CONV_WIDTH, C_CONV), CONV_WIDTH ** -0.5),
        'conv_b': nrm((DEPTH, C_CONV), 0.01),
        'conv_ln_g': gain((DEPTH, C_CONV)),
        'conv_ln_b': nrm((DEPTH, C_CONV), 0.01),
        'mla_q_norm': gain((DEPTH, MLA_Q_RANK)),
        'mla_w_uq': nrm((DEPTH, MLA_Q_RANK, MLA_HEADS, MLA_NOPE + MLA_ROPE), MLA_Q_RANK ** -0.5),
        'mla_kv_norm': gain((DEPTH, MLA_KV_RANK)),
        'mla_w_uk': nrm((DEPTH, MLA_KV_RANK, MLA_HEADS, MLA_NOPE), MLA_KV_RANK ** -0.5),
        'mla_w_uv': nrm((DEPTH, MLA_KV_RANK, MLA_HEADS, MLA_V), MLA_KV_RANK ** -0.5),
        'fox_f_bias': FOX_GATE_BIAS + nrm((DEPTH, FOX_HEADS), 0.1),
        'w_out': nrm((DEPTH, D_MIX, D_MODEL), D_MIX ** -0.5),
        'norm_xattn': gain((DEPTH, D_MODEL)),
        'norm_mem': gain((DEPTH, D_MODEL)),
        'xa_w_q': nrm((DEPTH, D_MODEL, XA_HEADS * XA_HEAD_DIM), D_MODEL ** -0.5),
        'xa_w_kv': nrm((DEPTH, D_MODEL, 2 * XA_HEADS * XA_HEAD_DIM), D_MODEL ** -0.5),
        'xa_w_o': nrm((DEPTH, XA_HEADS * XA_HEAD_DIM, D_MODEL), (XA_HEADS * XA_HEAD_DIM) ** -0.5),
        'norm_ffn': gain((DEPTH, D_MODEL)),
        'router_group_w': nrm((DEPTH, D_MODEL, N_GROUPS), D_MODEL ** -0.5),
        'router_group_b': nrm((DEPTH, N_GROUPS), 0.01),
        'router_expert_w': nrm((DEPTH, D_MODEL, N_EXPERTS), D_MODEL ** -0.5),
        'router_expert_b': nrm((DEPTH, N_EXPERTS), 0.01),
        'exp_w_gate': nrm((DEPTH, N_EXPERTS, D_MODEL, D_EXPERT), D_MODEL ** -0.5),
        'exp_w_up': nrm((DEPTH, N_EXPERTS, D_MODEL, D_EXPERT), D_MODEL ** -0.5),
        'exp_w_down': nrm((DEPTH, N_EXPERTS, D_EXPERT, D_MODEL), D_EXPERT ** -0.5),
        'norm_final': gain((D_MODEL,)),
    }


def reference(x_prompt, x_sample, cache_mla_ckv, cache_mla_krope, cache_fox_k, cache_fox_v, cache_fox_logf,
              cache_mem_k, cache_mem_v, state_conv, page_table, mem_prompt,
              norm_mix, w_in, conv_w, conv_b, conv_ln_g, conv_ln_b, mla_q_norm, mla_w_uq, mla_kv_norm,
              mla_w_uk, mla_w_uv, fox_f_bias, w_out, norm_xattn, norm_mem, xa_w_q, xa_w_kv, xa_w_o,
              norm_ffn, router_group_w, router_group_b, router_expert_w, router_expert_b,
              exp_w_gate, exp_w_up, exp_w_down, norm_final):
    s_len = x_prompt.shape[1]
    t_len = x_sample.shape[1]
    past = page_table.shape[1] * cache_mla_ckv.shape[2]
    pos_p = jnp.arange(s_len)
    pos_s = past + jnp.arange(t_len)
    xp, xs = x_prompt, x_sample
    p_ckv, p_kr, p_fk, p_fv, p_logf, p_mk, p_mv, p_conv = [], [], [], [], [], [], [], []
    s_ckv, s_kr, s_fk, s_fv, s_logf, s_conv = [], [], [], [], [], []
    for l in range(DEPTH):
        mw = (w_in[l], conv_w[l], conv_b[l], conv_ln_g[l], conv_ln_b[l], mla_q_norm[l], mla_w_uq[l],
              mla_kv_norm[l], mla_w_uk[l], mla_w_uv[l], fox_f_bias[l], w_out[l])
        yp, (ckv, kr, fk, fv, logf, cst) = mixer_prompt(rms_norm(xp, norm_mix[l]), pos_p, mw)
        p_ckv.append(ckv); p_kr.append(kr); p_fk.append(fk); p_fv.append(fv); p_logf.append(logf); p_conv.append(cst)
        ys, (ckv, kr, fk, fv, logf, cst) = mixer_sample(
            rms_norm(xs, norm_mix[l]), pos_s, mw, state_conv[l], cache_mla_ckv[l], cache_mla_krope[l],
            cache_fox_k[l], cache_fox_v[l], cache_fox_logf[l], page_table)
        s_ckv.append(ckv); s_kr.append(kr); s_fk.append(fk); s_fv.append(fv); s_logf.append(logf); s_conv.append(cst)
        xp = xp + yp
        xs = xs + ys
        mk, mv = memory_kv(mem_prompt, norm_mem[l], xa_w_kv[l])
        p_mk.append(mk); p_mv.append(mv)
        xp = xp + cross_attention(rms_norm(xp, norm_xattn[l]), mk, mv, xa_w_q[l], xa_w_o[l])
        xs = xs + cross_attention(rms_norm(xs, norm_xattn[l]), cache_mem_k[l], cache_mem_v[l], xa_w_q[l], xa_w_o[l])
        moe_w = (router_group_w[l], router_group_b[l], router_expert_w[l], router_expert_b[l],
                 exp_w_gate[l], exp_w_up[l], exp_w_down[l])
        xp = xp + hier_moe(rms_norm(xp, norm_ffn[l]), *moe_w)
        xs = xs + hier_moe(rms_norm(xs, norm_ffn[l]), *moe_w)
    return (rms_norm(xp, norm_final), rms_norm(xs, norm_final),
            jnp.stack(p_ckv), jnp.stack(p_kr), jnp.stack(p_fk), jnp.stack(p_fv), jnp.stack(p_logf),
            jnp.stack(p_mk), jnp.stack(p_mv), jnp.stack(p_conv),
            jnp.stack(s_ckv), jnp.stack(s_kr), jnp.stack(s_fk), jnp.stack(s_fv), jnp.stack(s_logf),
            jnp.stack(s_conv))
```

```python
import functools

import jax
import jax.numpy as jnp
from jax import lax
from jax.experimental import pallas as pl
from jax.experimental.pallas import tpu as pltpu

F32 = jnp.float32
BF16 = jnp.bfloat16

EPS = 1e-6
ROPE_THETA = 10000.0
C_CONV = 512
CONV_WIDTH = 31
MLA_HEADS = 6
MLA_NOPE = 128
MLA_ROPE = 64
MLA_Q_RANK = 512
MLA_KV_RANK = 256
FOX_HEADS = 6
FOX_HEAD_DIM = 128
XA_HEADS = 4
XA_HEAD_DIM = 128
N_GROUPS = 4
EXPERTS_PER_GROUP = 4
N_EXPERTS = 16
D_EXPERT = 512
MLA_SCALE = (MLA_NOPE + MLA_ROPE) ** -0.5
FOX_SCALE = FOX_HEAD_DIM ** -0.5
XA_SCALE = XA_HEAD_DIM ** -0.5

LANES = 128
SUBLANES = 8
VMEM_LIMIT_BYTES = 56 * 1024 * 1024

HEADS_PAD = SUBLANES
MLA_QK = 3 * LANES

_SEG_A = 0
_SEG_G = _SEG_A + C_CONV
_SEG_CQ = _SEG_G + C_CONV
_SEG_CKV = _SEG_CQ + MLA_Q_RANK
_SEG_FQ = _SEG_CKV + MLA_KV_RANK
_SEG_FK = _SEG_FQ + FOX_HEADS * FOX_HEAD_DIM
_SEG_FV = _SEG_FK + FOX_HEAD_DIM
_SEG_KR = _SEG_FV + FOX_HEAD_DIM
_SEG_KRS = _SEG_KR + LANES
_SEG_FF = _SEG_KRS + LANES
D_IN_EXT = _SEG_FF + LANES

_UQ_NOPE = 0
_UQ_ROPE = MLA_HEADS * MLA_NOPE
_UQ_ROPES = _UQ_ROPE + MLA_HEADS * LANES
D_UQ_EXT = _UQ_ROPES + MLA_HEADS * LANES

ROUTE_LANES = LANES


def _params(*sem):
    return pltpu.CompilerParams(dimension_semantics=sem, vmem_limit_bytes=VMEM_LIMIT_BYTES)


def _const_spec(shape):
    nd = len(shape)
    return pl.BlockSpec(shape, lambda *_: (0,) * nd, pipeline_mode=pl.Buffered(1))


def _rms(x, g):
    return x * lax.rsqrt(jnp.mean(x * x, axis=-1, keepdims=True) + EPS) * g


def _dot(a, b):
    return jnp.dot(a, b, preferred_element_type=F32)


def _dot_nt(a, b):
    return lax.dot_general(a, b, (((1,), (1,)), ((), ())), preferred_element_type=F32)


def _sigmoid(x):
    return 1.0 / (1.0 + jnp.exp(-x))


def _silu(x):
    return x * _sigmoid(x)


def _log_sigmoid(z):
    return jnp.minimum(z, 0.0) - jnp.log(1.0 + jnp.exp(-jnp.abs(z)))


def _split3(x):
    hi = x.astype(BF16)
    r1 = x - hi.astype(F32)
    mid = r1.astype(BF16)
    lo = (r1 - mid.astype(F32)).astype(BF16)
    return hi, mid, lo


def _proj_kernel(has_y, *refs):
    if has_y:
        x_ref, y0_ref, y1_ref = refs[:3]
        refs = refs[3:]
    else:
        x_ref = refs[0]
        refs = refs[1:]
    (g_ref, win_ref, wuq_ref, wukt_ref, qn_ref, kvn_ref, fb_ref, cos_ref, sin_ref,
     xo_ref, u_ref, ckv_ref, kr_ref, fk_ref, fv_ref, logf_ref,
     kcat_ref, qcat_ref, fqb_ref, fkb_ref, fvb_ref) = refs

    x = x_ref[...]
    if has_y:
        x = x + y0_ref[...] + y1_ref[...]
    xo_ref[...] = x
    xb = _rms(x, g_ref[...]).astype(BF16)

    def seg(lo, width):
        return _dot(xb, win_ref[:, lo:lo + width])

    u_ref[...] = seg(_SEG_A, C_CONV) * _sigmoid(seg(_SEG_G, C_CONV))
    ckv = _rms(seg(_SEG_CKV, MLA_KV_RANK), kvn_ref[...])
    ckv_ref[...] = ckv
    cos = cos_ref[...]
    sin = sin_ref[...]
    kr = seg(_SEG_KR, LANES) * cos + seg(_SEG_KRS, LANES) * sin
    kr_ref[...] = kr[:, :MLA_ROPE]
    kcat_ref[:, :MLA_KV_RANK] = ckv.astype(BF16)
    kcat_ref[:, MLA_KV_RANK:] = kr.astype(BF16)
    cqn = _rms(seg(_SEG_CQ, MLA_Q_RANK), qn_ref[...]).astype(BF16)
    q = _dot(cqn, wuq_ref[...])
    for h in range(MLA_HEADS):
        qn = q[:, _UQ_NOPE + h * MLA_NOPE:_UQ_NOPE + (h + 1) * MLA_NOPE].astype(BF16)
        ql = _dot(qn, wukt_ref[h]) * MLA_SCALE
        qr = (q[:, _UQ_ROPE + h * LANES:_UQ_ROPE + (h + 1) * LANES] * cos
              + q[:, _UQ_ROPES + h * LANES:_UQ_ROPES + (h + 1) * LANES] * sin) * MLA_SCALE
        qcat_ref[:, h * MLA_QK:h * MLA_QK + MLA_KV_RANK] = ql.astype(BF16)
        qcat_ref[:, h * MLA_QK + MLA_KV_RANK:(h + 1) * MLA_QK] = qr.astype(BF16)
    fqb_ref[...] = (seg(_SEG_FQ, FOX_HEADS * FOX_HEAD_DIM) * FOX_SCALE).astype(BF16)
    fk = seg(_SEG_FK, FOX_HEAD_DIM)
    fv = seg(_SEG_FV, FOX_HEAD_DIM)
    fk_ref[...] = fk
    fv_ref[...] = fv
    fkb_ref[...] = fk.astype(BF16)
    fvb_ref[...] = fv.astype(BF16)
    logf = _log_sigmoid(seg(_SEG_FF, LANES) + fb_ref[...])
    logf_ref[...] = logf[:, :FOX_HEADS]


def _proj(x, ypair, g, w_in_ext, w_uq_ext, w_ukt, q_norm, kv_norm, f_bias_row, cos_t, sin_t, tm):
    t, d = x.shape
    nblk = t // tm
    has_y = ypair is not None
    row = lambda w: pl.BlockSpec((tm, w), lambda i: (i, 0))
    in_specs = [row(d)]
    args = [x]
    if has_y:
        in_specs += [row(d), pl.BlockSpec((tm, d), lambda i: (i + nblk, 0))]
        args += [ypair, ypair]
    in_specs += [_const_spec((1, d)), _const_spec(w_in_ext.shape), _const_spec(w_uq_ext.shape),
                 _const_spec(w_ukt.shape), _const_spec((1, MLA_Q_RANK)), _const_spec((1, MLA_KV_RANK)),
                 _const_spec((1, LANES)), row(LANES), row(LANES)]
    args += [g, w_in_ext, w_uq_ext, w_ukt, q_norm, kv_norm, f_bias_row, cos_t, sin_t]
    widths = [(d, F32), (C_CONV, F32), (MLA_KV_RANK, F32), (MLA_ROPE, F32), (FOX_HEAD_DIM, F32),
              (FOX_HEAD_DIM, F32), (FOX_HEADS, F32), (MLA_QK, BF16), (MLA_HEADS * MLA_QK, BF16),
              (FOX_HEADS * FOX_HEAD_DIM, BF16), (FOX_HEAD_DIM, BF16), (FOX_HEAD_DIM, BF16)]
    return pl.pallas_call(
        functools.partial(_proj_kernel, has_y),
        grid=(nblk,),
        in_specs=in_specs,
        out_specs=[row(w) for w, _ in widths],
        out_shape=[jax.ShapeDtypeStruct((t, w), dt) for w, dt in widths],
        compiler_params=_params("parallel"),
        name="proj",
    )(*args)


def _cumsum_kernel(lf_ref, tri_ref, o_ref):
    n_chunks = lf_ref.shape[2] // LANES
    tri = tri_ref[...]
    carry = jnp.zeros((HEADS_PAD, 1), F32)
    for c in range(n_chunks):
        hi, mid, lo = _split3(lf_ref[0, :, c * LANES:(c + 1) * LANES])
        local = _dot(hi, tri) + _dot(mid, tri) + _dot(lo, tri)
        cum = local + carry
        o_ref[0, :, c * LANES:(c + 1) * LANES] = cum
        carry = cum[:, LANES - 1:LANES]


def _cumsum_time(lf_t):
    b, hp, s = lf_t.shape
    tri = (jnp.arange(LANES)[:, None] <= jnp.arange(LANES)[None, :]).astype(BF16)
    return pl.pallas_call(
        _cumsum_kernel,
        grid=(b,),
        in_specs=[pl.BlockSpec((1, hp, s), lambda i: (i, 0, 0)), _const_spec((LANES, LANES))],
        out_specs=pl.BlockSpec((1, hp, s), lambda i: (i, 0, 0)),
        out_shape=jax.ShapeDtypeStruct((b, hp, s), F32),
        compiler_params=_params("parallel"),
        name="cumsum",
    )(lf_t, tri)


_CONV_HALO = 32
_CONV_ROWS = 64


def _layer_norm_silu(y, g, b):
    mu = jnp.mean(y, axis=-1, keepdims=True)
    yc = y - mu
    var = jnp.mean(yc * yc, axis=-1, keepdims=True)
    return _silu(yc * lax.rsqrt(var + EPS) * g + b)


def _conv_prompt_kernel(blocks_per_seq, u_ref, halo_ref, w_ref, b_ref, g_ref, be_ref, o_ref, ext_ref):
    tm = u_ref.shape[0]
    first = (pl.program_id(0) % blocks_per_seq) == 0
    ext_ref[:_CONV_HALO, :] = jnp.where(first, 0.0, halo_ref[...])
    ext_ref[_CONV_HALO:, :] = u_ref[...]
    off = _CONV_HALO - (CONV_WIDTH - 1)
    for r0 in range(0, tm, _CONV_ROWS):
        acc = jnp.zeros((_CONV_ROWS, C_CONV), F32) + b_ref[...]
        for k in range(CONV_WIDTH):
            acc = acc + ext_ref[r0 + k + off:r0 + k + off + _CONV_ROWS, :] * w_ref[k:k + 1, :]
        o_ref[r0:r0 + _CONV_ROWS, :] = _layer_norm_silu(acc, g_ref[...], be_ref[...]).astype(o_ref.dtype)


def _conv_prompt(u, batch, seq, conv_w, conv_b, ln_g, ln_b, tm):
    n_rows = batch * seq
    nblk = n_rows // tm
    per_halo = tm // _CONV_HALO
    return pl.pallas_call(
        functools.partial(_conv_prompt_kernel, seq // tm),
        grid=(nblk,),
        in_specs=[pl.BlockSpec((tm, C_CONV), lambda i: (i, 0)),
                  pl.BlockSpec((_CONV_HALO, C_CONV), lambda i: (jnp.maximum(i * per_halo - 1, 0), 0)),
                  _const_spec((CONV_WIDTH, C_CONV)), _const_spec((1, C_CONV)),
                  _const_spec((1, C_CONV)), _const_spec((1, C_CONV))],
        out_specs=pl.BlockSpec((tm, C_CONV), lambda i: (i, 0)),
        out_shape=jax.ShapeDtypeStruct((n_rows, C_CONV), BF16),
        scratch_shapes=[pltpu.VMEM((tm + _CONV_HALO, C_CONV), F32)],
        compiler_params=_params("parallel"),
        name="conv_prompt",
    )(u, u, conv_w, conv_b, ln_g, ln_b)


def _conv_sample_kernel(st_ref, u_ref, w_ref, b_ref, g_ref, be_ref, o_ref):
    w = w_ref[...]
    hist = jnp.sum(st_ref[...] * w[None, :CONV_WIDTH - 1, :], axis=1)
    y = hist + u_ref[...] * w[CONV_WIDTH - 1:CONV_WIDTH, :] + b_ref[...]
    o_ref[...] = _layer_norm_silu(y, g_ref[...], be_ref[...]).astype(o_ref.dtype)


def _conv_sample(state, u_s, conv_w, conv_b, ln_g, ln_b, nb):
    n = state.shape[0]
    return pl.pallas_call(
        _conv_sample_kernel,
        grid=(n // nb,),
        in_specs=[pl.BlockSpec((nb, CONV_WIDTH - 1, C_CONV), lambda i: (i, 0, 0)),
                  pl.BlockSpec((nb, C_CONV), lambda i: (i, 0)),
                  _const_spec((CONV_WIDTH, C_CONV)), _const_spec((1, C_CONV)),
                  _const_spec((1, C_CONV)), _const_spec((1, C_CONV))],
        out_specs=pl.BlockSpec((nb, C_CONV), lambda i: (i, 0)),
        out_shape=jax.ShapeDtypeStruct((n, C_CONV), BF16),
        compiler_params=_params("parallel"),
        name="conv_sample",
    )(state, u_s, conv_w, conv_b, ln_g, ln_b)


def _flash_kernel(n_heads, dq, dv, has_bias, *refs):
    if has_bias:
        q_ref, k_ref, v_ref, ci_ref, cs_ref, o_ref, m_ref, l_ref, acc_ref = refs
    else:
        q_ref, k_ref, v_ref, o_ref, m_ref, l_ref, acc_ref = refs
    i = pl.program_id(1)
    j = pl.program_id(2)
    tq = q_ref.shape[0]
    tk = k_ref.shape[0]

    @pl.when(j == 0)
    def _():
        m_ref[...] = jnp.full(m_ref.shape, -jnp.inf, F32)
        l_ref[...] = jnp.zeros(l_ref.shape, F32)
        acc_ref[...] = jnp.zeros(acc_ref.shape, F32)

    def update(masked):
        k = k_ref[...]
        v = v_ref[...]
        if masked:
            keep = (lax.broadcasted_iota(jnp.int32, (tq, tk), 1)
                    <= lax.broadcasted_iota(jnp.int32, (tq, tk), 0))
        for h in range(n_heads):
            s = _dot_nt(q_ref[:, h * dq:(h + 1) * dq], k)
            if has_bias:
                s = s + (ci_ref[:, h:h + 1] - cs_ref[0, h:h + 1, :])
            if masked:
                s = jnp.where(keep, s, -jnp.inf)
            m_prev = m_ref[h]
            m_new = jnp.maximum(m_prev, jnp.max(s, axis=-1, keepdims=True))
            alpha = jnp.exp(m_prev - m_new)
            p = jnp.exp(s - m_new)
            l_ref[h] = alpha * l_ref[h] + jnp.sum(p, axis=-1, keepdims=True)
            acc_ref[h] = alpha * acc_ref[h] + _dot(p.astype(BF16), v)
            m_ref[h] = m_new

    @pl.when(j < i)
    def _():
        update(False)

    @pl.when(j == i)
    def _():
        update(True)
        for h in range(n_heads):
            o_ref[:, h * dv:(h + 1) * dv] = (acc_ref[h] / l_ref[h]).astype(o_ref.dtype)


def _flash(q, k, v, ci, cs, batch, seq, n_heads, dq, dv, tq):
    nq = seq // tq
    has_bias = ci is not None
    in_specs = [pl.BlockSpec((tq, n_heads * dq), lambda b, i, j: (b * nq + i, 0)),
                pl.BlockSpec((tq, dq), lambda b, i, j: (b * nq + jnp.minimum(i, j), 0)),
                pl.BlockSpec((tq, dv), lambda b, i, j: (b * nq + jnp.minimum(i, j), 0))]
    args = [q, k, v]
    if has_bias:
        in_specs += [pl.BlockSpec((tq, HEADS_PAD), lambda b, i, j: (b * nq + i, 0)),
                     pl.BlockSpec((1, HEADS_PAD, tq), lambda b, i, j: (b, 0, jnp.minimum(i, j)))]
        args += [ci, cs]
    return pl.pallas_call(
        functools.partial(_flash_kernel, n_heads, dq, dv, has_bias),
        grid=(batch, nq, nq),
        in_specs=in_specs,
        out_specs=pl.BlockSpec((tq, n_heads * dv), lambda b, i, j: (b * nq + i, 0)),
        out_shape=jax.ShapeDtypeStruct((batch * seq, n_heads * dv), BF16),
        scratch_shapes=[pltpu.VMEM((n_heads, tq, 1), F32), pltpu.VMEM((n_heads, tq, 1), F32),
                        pltpu.VMEM((n_heads, tq, dv), F32)],
        compiler_params=_params("parallel", "parallel", "arbitrary"),
        name="flash_bias" if has_bias else "flash",
    )(*args)


_PAGES_PER_STEP = 8


def _decode_kernel(n_pg, pt_ref, q_ref, fq_ref, knew_ref, fknew_ref, fvnew_ref, cn_ref, *refs):
    ckv_refs = refs[0 * n_pg:1 * n_pg]
    kr_refs = refs[1 * n_pg:2 * n_pg]
    fk_refs = refs[2 * n_pg:3 * n_pg]
    fv_refs = refs[3 * n_pg:4 * n_pg]
    lf_refs = refs[4 * n_pg:5 * n_pg]
    (ml_ref, fo_ref, m1_ref, l1_ref, a1_ref, m2_ref, l2_ref, a2_ref, carry_ref) = refs[5 * n_pg:]
    c = pl.program_id(1)
    last = pl.num_programs(1) - 1
    page = ckv_refs[0].shape[2]

    @pl.when(c == 0)
    def _():
        m1_ref[...] = jnp.full(m1_ref.shape, -jnp.inf, F32)
        m2_ref[...] = jnp.full(m2_ref.shape, -jnp.inf, F32)
        l1_ref[...] = jnp.zeros(l1_ref.shape, F32)
        l2_ref[...] = jnp.zeros(l2_ref.shape, F32)
        a1_ref[...] = jnp.zeros(a1_ref.shape, F32)
        a2_ref[...] = jnp.zeros(a2_ref.shape, F32)
        carry_ref[...] = jnp.zeros(carry_ref.shape, F32)

    q = q_ref[0]
    q_lat = q[:, :MLA_KV_RANK]
    q_rope = q[:, MLA_KV_RANK:MLA_KV_RANK + MLA_ROPE]
    fq = fq_ref[0]
    cn = cn_ref[0]
    lane = lax.broadcasted_iota(jnp.int32, (HEADS_PAD, page), 1)

    carry = carry_ref[...][:, :1]
    s1, s2, v1, v2 = [None] * n_pg, [None] * n_pg, [None] * n_pg, [None] * n_pg
    for g in reversed(range(n_pg)):
        ckv = ckv_refs[g][0, 0].astype(BF16)
        v1[g] = ckv
        s1[g] = _dot_nt(q_lat, ckv) + _dot_nt(q_rope, kr_refs[g][0, 0].astype(BF16))
        lf = lf_refs[g][0, 0]
        incl = lf
        d = 1
        while d < page:
            incl = incl + jnp.where(lane + d < page, pltpu.roll(incl, page - d, 1), 0.0)
            d *= 2
        s2[g] = _dot_nt(fq, fk_refs[g][0, 0].astype(BF16)) + ((incl - lf) + carry + cn)
        v2[g] = fv_refs[g][0, 0].astype(BF16)
        carry = carry + incl[:, :1]
    carry_ref[...] = jnp.broadcast_to(carry, carry_ref.shape)

    def online(s_parts, v_parts, m_ref, l_ref, a_ref):
        s = jnp.concatenate(s_parts, axis=1)
        v = jnp.concatenate(v_parts, axis=0)
        m_prev = m_ref[...]
        m_new = jnp.maximum(m_prev, jnp.max(s, axis=-1, keepdims=True))
        alpha = jnp.exp(m_prev - m_new)
        p = jnp.exp(s - m_new)
        l_ref[...] = alpha * l_ref[...] + jnp.sum(p, axis=-1, keepdims=True)
        a_ref[...] = alpha * a_ref[...] + _dot(p.astype(BF16), v)
        m_ref[...] = m_new

    online(s1, v1, m1_ref, l1_ref, a1_ref)
    online(s2, v2, m2_ref, l2_ref, a2_ref)

    @pl.when(c == last)
    def _():
        def finish(s_new, v_new, m_ref, l_ref, a_ref, o_ref):
            m_prev = m_ref[...]
            m_new = jnp.maximum(m_prev, s_new)
            alpha = jnp.exp(m_prev - m_new)
            p = jnp.exp(s_new - m_new)
            l = alpha * l_ref[...] + p
            o_ref[0] = ((alpha * a_ref[...] + p * v_new) / l).astype(o_ref.dtype)

        k_new = knew_ref[0].astype(F32)
        s_new = jnp.sum(q.astype(F32) * k_new, axis=-1, keepdims=True)
        finish(s_new, k_new[:, :MLA_KV_RANK], m1_ref, l1_ref, a1_ref, ml_ref)
        s_new = jnp.sum(fq.astype(F32) * fknew_ref[0].astype(F32), axis=-1, keepdims=True)
        finish(s_new, fvnew_ref[0].astype(F32), m2_ref, l2_ref, a2_ref, fo_ref)


def _decode(layer, page_table, q_s, fq_s, k_new, fk_new, fv_new, cn, ckv_pool, kr_pool, fk_pool, fv_pool, lf_pool_t):
    n, n_pages = page_table.shape
    page = ckv_pool.shape[2]
    n_pg = _PAGES_PER_STEP
    n_chunks = n_pages // n_pg
    pt_flat = page_table.reshape(-1)

    def per_sample(shape):
        return pl.BlockSpec((1,) + shape, lambda i, c, pt: (i, 0, 0))

    def page_spec(rows, width, g):
        def index(i, c, pt):
            return (layer, pt[i * n_pages + (n_chunks - 1 - c) * n_pg + g], 0, 0)
        return pl.BlockSpec((1, 1, rows, width), index)

    in_specs = [per_sample((HEADS_PAD, MLA_QK)), per_sample((HEADS_PAD, FOX_HEAD_DIM)),
                per_sample((1, MLA_QK)), per_sample((1, FOX_HEAD_DIM)), per_sample((1, FOX_HEAD_DIM)),
                per_sample((HEADS_PAD, 1))]
    args = [q_s, fq_s, k_new, fk_new, fv_new, cn]
    for pool, rows, width in ((ckv_pool, page, MLA_KV_RANK), (kr_pool, page, MLA_ROPE),
                              (fk_pool, page, FOX_HEAD_DIM), (fv_pool, page, FOX_HEAD_DIM),
                              (lf_pool_t, HEADS_PAD, page)):
        for g in range(n_pg):
            in_specs.append(page_spec(rows, width, g))
            args.append(pool)
    grid_spec = pltpu.PrefetchScalarGridSpec(
        num_scalar_prefetch=1,
        grid=(n, n_chunks),
        in_specs=in_specs,
        out_specs=[pl.BlockSpec((1, HEADS_PAD, MLA_KV_RANK), lambda i, c, pt: (i, 0, 0)),
                   pl.BlockSpec((1, HEADS_PAD, FOX_HEAD_DIM), lambda i, c, pt: (i, 0, 0))],
        scratch_shapes=[pltpu.VMEM((HEADS_PAD, 1), F32), pltpu.VMEM((HEADS_PAD, 1), F32),
                        pltpu.VMEM((HEADS_PAD, MLA_KV_RANK), F32),
                        pltpu.VMEM((HEADS_PAD, 1), F32), pltpu.VMEM((HEADS_PAD, 1), F32),
                        pltpu.VMEM((HEADS_PAD, FOX_HEAD_DIM), F32),
                        pltpu.VMEM((HEADS_PAD, LANES), F32)])
    return pl.pallas_call(
        functools.partial(_decode_kernel, n_pg),
        grid_spec=grid_spec,
        out_shape=[jax.ShapeDtypeStruct((n, HEADS_PAD, MLA_KV_RANK), BF16),
                   jax.ShapeDtypeStruct((n, HEADS_PAD, FOX_HEAD_DIM), BF16)],
        compiler_params=_params("parallel", "arbitrary"),
        name="decode",
    )(pt_flat, *args)


def _merge_kernel(x_ref, conv_ref, lat_ref, fox_ref, wuv_ref, wout_ref, g_ref, wq_ref,
                  xo_ref, qx_ref, cat_ref):
    cat_ref[:, :C_CONV] = conv_ref[...]
    for h in range(MLA_HEADS):
        o = _dot(lat_ref[:, h * MLA_KV_RANK:(h + 1) * MLA_KV_RANK], wuv_ref[h])
        cat_ref[:, C_CONV + h * MLA_NOPE:C_CONV + (h + 1) * MLA_NOPE] = o.astype(BF16)
    base = C_CONV + MLA_HEADS * MLA_NOPE
    cat_ref[:, base:] = fox_ref[...]
    x = x_ref[...] + _dot(cat_ref[...], wout_ref[...])
    xo_ref[...] = x
    xb = _rms(x, g_ref[...]).astype(BF16)
    qx_ref[...] = (_dot(xb, wq_ref[...]) * XA_SCALE).astype(BF16)


def _merge(x, conv_o, lat, fox_o, w_uv, w_out, g, w_q, tm):
    t, d = x.shape
    row = lambda w: pl.BlockSpec((tm, w), lambda i: (i, 0))
    dq = w_q.shape[1]
    return pl.pallas_call(
        _merge_kernel,
        grid=(t // tm,),
        in_specs=[row(d), row(conv_o.shape[1]), row(lat.shape[1]), row(fox_o.shape[1]),
                  _const_spec(w_uv.shape), _const_spec(w_out.shape), _const_spec((1, d)), _const_spec(w_q.shape)],
        out_specs=[row(d), row(dq)],
        out_shape=[jax.ShapeDtypeStruct((t, d), F32), jax.ShapeDtypeStruct((t, dq), BF16)],
        scratch_shapes=[pltpu.VMEM((tm, w_out.shape[0]), BF16)],
        compiler_params=_params("parallel"),
        name="merge",
    )(x, conv_o, lat, fox_o, w_uv, w_out, g, w_q)


def _memkv_kernel(mem_ref, g_ref, w_ref, k_ref, v_ref, kb_ref, vb_ref):
    kv = _dot(_rms(mem_ref[...], g_ref[0]).astype(BF16), w_ref[0])
    half = kv.shape[1] // 2
    k_ref[0] = kv[:, :half]
    v_ref[0] = kv[:, half:]
    kb_ref[0] = kv[:, :half].astype(BF16)
    vb_ref[0] = kv[:, half:].astype(BF16)


def _memkv(mem, norm_mem, w_kv, tm):
    m, d = mem.shape
    depth, _, two_e = w_kv.shape
    e = two_e // 2
    out = pl.BlockSpec((1, tm, e), lambda l, i: (l, i, 0))
    return pl.pallas_call(
        _memkv_kernel,
        grid=(depth, m // tm),
        in_specs=[pl.BlockSpec((tm, d), lambda l, i: (i, 0)),
                  pl.BlockSpec((1, 1, d), lambda l, i: (l, 0, 0)),
                  pl.BlockSpec((1, d, two_e), lambda l, i: (l, 0, 0))],
        out_specs=[out, out, out, out],
        out_shape=[jax.ShapeDtypeStruct((depth, m, e), F32), jax.ShapeDtypeStruct((depth, m, e), F32),
                   jax.ShapeDtypeStruct((depth, m, e), BF16), jax.ShapeDtypeStruct((depth, m, e), BF16)],
        compiler_params=_params("parallel", "parallel"),
        name="memkv",
    )(mem, norm_mem, w_kv)


def _xattn_prompt_kernel(q_ref, k_ref, v_ref, o_ref):
    for h in range(XA_HEADS):
        sl = slice(h * XA_HEAD_DIM, (h + 1) * XA_HEAD_DIM)
        s = _dot_nt(q_ref[:, sl], k_ref[0, 0, :, sl])
        p = jnp.exp(s - jnp.max(s, axis=-1, keepdims=True))
        p = p / jnp.sum(p, axis=-1, keepdims=True)
        o_ref[:, sl] = _dot(p.astype(BF16), v_ref[0, 0, :, sl]).astype(o_ref.dtype)


def _xattn_prompt(layer, qx, mk_b, mv_b, batch, seq, tm):
    e = qx.shape[1]
    n_mem = mk_b.shape[2]
    per = seq // tm
    kv_spec = pl.BlockSpec((1, 1, n_mem, e), lambda i: (layer, i // per, 0, 0))
    return pl.pallas_call(
        _xattn_prompt_kernel,
        grid=(batch * per,),
        in_specs=[pl.BlockSpec((tm, e), lambda i: (i, 0)), kv_spec, kv_spec],
        out_specs=pl.BlockSpec((tm, e), lambda i: (i, 0)),
        out_shape=jax.ShapeDtypeStruct((batch * seq, e), BF16),
        compiler_params=_params("parallel"),
        name="xattn_prompt",
    )(qx, mk_b, mv_b)


def _xattn_sample_kernel(q_ref, k_ref, v_ref, o_ref):
    nb = q_ref.shape[0]
    q = q_ref[...].astype(F32)
    k = k_ref[0].astype(BF16).astype(F32)
    v = v_ref[0].astype(BF16).astype(F32)
    prod = q * k
    for h in range(XA_HEADS):
        sl = slice(h * XA_HEAD_DIM, (h + 1) * XA_HEAD_DIM)
        s = jnp.sum(prod[:, :, sl], axis=-1, keepdims=True)
        p = jnp.exp(s - jnp.max(s, axis=1, keepdims=True))
        p = p / jnp.sum(p, axis=1, keepdims=True)
        p = p.astype(BF16).astype(F32)
        o_ref[:, :, sl] = jnp.sum(p * v[:, :, sl], axis=1, keepdims=True).astype(o_ref.dtype)
    del nb


def _xattn_sample(layer, qx_s, mem_k, mem_v, nb):
    n, _, e = qx_s.shape
    n_mem = mem_k.shape[2]
    kv_spec = pl.BlockSpec((1, nb, n_mem, e), lambda i: (layer, i, 0, 0))
    return pl.pallas_call(
        _xattn_sample_kernel,
        grid=(n // nb,),
        in_specs=[pl.BlockSpec((nb, 1, e), lambda i: (i, 0, 0)), kv_spec, kv_spec],
        out_specs=pl.BlockSpec((nb, 1, e), lambda i: (i, 0, 0)),
        out_shape=jax.ShapeDtypeStruct((n, 1, e), BF16),
        compiler_params=_params("parallel"),
        name="xattn_sample",
    )(qx_s, mem_k, mem_v)


def _route_kernel(x_ref, o_ref, wo_ref, g_ref, wr_hi_ref, wr_lo_ref, br_ref,
                  xo_ref, xn_ref, route_ref):
    x = x_ref[...] + _dot(o_ref[...], wo_ref[...])
    xo_ref[...] = x
    xn = _rms(x, g_ref[...])
    xn_ref[...] = xn
    hi = xn.astype(BF16)
    lo = (xn - hi.astype(F32)).astype(BF16)
    logits = (_dot(hi, wr_hi_ref[...]) + _dot(lo, wr_hi_ref[...]) + _dot(hi, wr_lo_ref[...])) + br_ref[...]

    lane = lax.broadcasted_iota(jnp.int32, logits.shape, 1).astype(F32)
    neg = -jnp.inf
    big = float(ROUTE_LANES)

    def top1(vals):
        vmax = jnp.max(vals, axis=-1, keepdims=True)
        idx = jnp.min(jnp.where(vals == vmax, lane, big), axis=-1, keepdims=True)
        return vmax, idx

    g_vals = jnp.where(lane < N_GROUPS, logits, neg)
    g_max, g_sel = top1(g_vals)
    g_prob = 1.0 / jnp.sum(jnp.exp(g_vals - g_max), axis=-1, keepdims=True)
    e_lo = N_GROUPS + g_sel * EXPERTS_PER_GROUP
    e_vals = jnp.where((lane >= e_lo) & (lane < e_lo + EXPERTS_PER_GROUP), logits, neg)
    v1, i1 = top1(e_vals)
    v2, i2 = top1(jnp.where(lane == i1, neg, e_vals))
    r = jnp.exp(v2 - v1)
    w1 = g_prob / (1.0 + r)
    w2 = g_prob * r / (1.0 + r)
    rec = jnp.where(lane == 0, i1 - N_GROUPS, 0.0)
    rec = jnp.where(lane == 1, i2 - N_GROUPS, rec)
    rec = jnp.where(lane == 2, w1, rec)
    rec = jnp.where(lane == 3, w2, rec)
    route_ref[...] = rec


def _route(x, o, w_o, g, wr_hi, wr_lo, b_r, tm):
    t, d = x.shape
    row = lambda w: pl.BlockSpec((tm, w), lambda i: (i, 0))
    return pl.pallas_call(
        _route_kernel,
        grid=(t // tm,),
        in_specs=[row(d), row(o.shape[1]), _const_spec(w_o.shape), _const_spec((1, d)),
                  _const_spec(wr_hi.shape), _const_spec(wr_lo.shape), _const_spec((1, ROUTE_LANES))],
        out_specs=[row(d), row(d), row(ROUTE_LANES)],
        out_shape=[jax.ShapeDtypeStruct((t, d), F32), jax.ShapeDtypeStruct((t, d), F32),
                   jax.ShapeDtypeStruct((t, ROUTE_LANES), F32)],
        compiler_params=_params("parallel"),
        name="route",
    )(x, o, w_o, g, wr_hi, wr_lo, b_r)


_MOE_ROWS = 256


def _moe_kernel(tok_ref, dst_ref, tile_e_ref, nt_ref, gate_ref, x_hbm, wg_ref, wu_ref, wd_ref,
                out_hbm, xbuf, ybuf, gsem, ssem):
    del tile_e_ref
    tm = ybuf.shape[0]
    i = pl.program_id(0)
    nt = nt_ref[0]

    def gather_copy(tile, slot, r):
        tok = tok_ref[tile * tm + r]
        return pltpu.make_async_copy(x_hbm.at[pl.ds(tok, 1)], xbuf.at[slot, pl.ds(r, 1)], gsem.at[slot])

    def scatter_copy(tile, r):
        dst = dst_ref[tile * tm + r]
        return pltpu.make_async_copy(ybuf.at[pl.ds(r, 1)], out_hbm.at[pl.ds(dst, 1)], ssem.at[0])

    def for_rows(fn):
        def body(r, carry):
            fn(r)
            return carry
        lax.fori_loop(0, tm, body, 0)

    @pl.when(i == 0)
    def _():
        for_rows(lambda r: gather_copy(0, 0, r).start())

    @pl.when(i + 1 < nt)
    def _():
        for_rows(lambda r: gather_copy(i + 1, (i + 1) % 2, r).start())

    @pl.when(i < nt)
    def _():
        slot = i % 2
        for_rows(lambda r: gather_copy(i, slot, r).wait())
        x = xbuf[slot].astype(BF16)
        h = _silu(_dot(x, wg_ref[0])) * _dot(x, wu_ref[0])
        y = _dot((h * gate_ref[...]).astype(BF16), wd_ref[0])

        @pl.when(i > 0)
        def _():
            for_rows(lambda r: scatter_copy(i - 1, r).wait())

        ybuf[...] = y
        for_rows(lambda r: scatter_copy(i, r).start())

    @pl.when(i == nt - 1)
    def _():
        for_rows(lambda r: scatter_copy(i, r).wait())


def _moe(xn, tok, dst, tile_e, n_tiles, gate_rows, w_gate, w_up, w_down):
    t, d = xn.shape
    tm = _MOE_ROWS
    r_max = tok.shape[0]
    f = w_gate.shape[2]

    def w_index(i, tok_r, dst_r, tile_e_r, nt_r):
        return (tile_e_r[jnp.minimum(i, nt_r[0] - 1)], 0, 0)

    grid_spec = pltpu.PrefetchScalarGridSpec(
        num_scalar_prefetch=4,
        grid=(r_max // tm,),
        in_specs=[pl.BlockSpec((tm, 1), lambda i, *_: (i, 0)),
                  pl.BlockSpec(memory_space=pl.ANY),
                  pl.BlockSpec((1, d, f), w_index), pl.BlockSpec((1, d, f), w_index),
                  pl.BlockSpec((1, f, d), w_index)],
        out_specs=pl.BlockSpec(memory_space=pl.ANY),
        scratch_shapes=[pltpu.VMEM((2, tm, d), F32), pltpu.VMEM((tm, d), F32),
                        pltpu.SemaphoreType.DMA((2,)), pltpu.SemaphoreType.DMA((1,))])
    return pl.pallas_call(
        _moe_kernel,
        grid_spec=grid_spec,
        out_shape=jax.ShapeDtypeStruct((r_max, d), F32),
        compiler_params=_params("arbitrary"),
        name="moe",
    )(tok, dst, tile_e, n_tiles, gate_rows, xn, w_gate, w_up, w_down)


def _moe_rows_max(t):
    tm = _MOE_ROWS
    return ((2 * t + N_EXPERTS * (tm - 1)) // tm) * tm


def _moe_plan(route, t):
    tm = _MOE_ROWS
    r_max = _moe_rows_max(t)
    e_flat = route[:, :2].astype(jnp.int32).reshape(-1)
    w_flat = route[:, 2:4].reshape(-1)
    onehot = (e_flat[:, None] == jnp.arange(N_EXPERTS)[None, :]).astype(jnp.int32)
    ranks = jnp.cumsum(onehot, axis=0)
    counts = ranks[-1]
    rank = jnp.sum(ranks * onehot, axis=1) - 1
    padded = (counts + tm - 1) // tm * tm
    ends = jnp.cumsum(padded)
    pos = (ends - padded)[e_flat] + rank
    n_rows = ends[-1]
    pair = jnp.arange(2 * t, dtype=jnp.int32)
    row_pair = jnp.full((r_max,), -1, jnp.int32).at[pos].set(pair)
    valid = row_pair >= 0
    tok = jnp.where(valid, row_pair // 2, 0)
    n_pad_before = jnp.cumsum((~valid).astype(jnp.int32)) - 1
    dst = jnp.where(valid, (row_pair % 2) * t + row_pair // 2, 2 * t + n_pad_before)
    gate_rows = jnp.where(valid, w_flat[jnp.maximum(row_pair, 0)], 0.0)[:, None]
    tile_start = jnp.arange(r_max // tm, dtype=jnp.int32) * tm
    tile_e = jnp.minimum(jnp.searchsorted(ends, tile_start, side="right"), N_EXPERTS - 1).astype(jnp.int32)
    n_tiles = (n_rows // tm).astype(jnp.int32).reshape(1)
    return tok.astype(jnp.int32), dst.astype(jnp.int32), tile_e, n_tiles, gate_rows


def _final_kernel(x_ref, y0_ref, y1_ref, g_ref, o_ref):
    o_ref[...] = _rms(x_ref[...] + y0_ref[...] + y1_ref[...], g_ref[...])


def _final(x, ypair, g, tm):
    t, d = x.shape
    nblk = t // tm
    row = pl.BlockSpec((tm, d), lambda i: (i, 0))
    return pl.pallas_call(
        _final_kernel,
        grid=(nblk,),
        in_specs=[row, row, pl.BlockSpec((tm, d), lambda i: (i + nblk, 0)), _const_spec((1, d))],
        out_specs=row,
        out_shape=jax.ShapeDtypeStruct((t, d), F32),
        compiler_params=_params("parallel"),
        name="final_norm",
    )(x, ypair, ypair, g)


def _rope_tables(pos):
    half = MLA_ROPE // 2
    inv = ROPE_THETA ** (-jnp.arange(half, dtype=F32) / half)
    ang = pos.astype(F32)[:, None] * inv[None, :]
    cos, sin = jnp.cos(ang), jnp.sin(ang)
    pad = jnp.zeros((pos.shape[0], LANES - MLA_ROPE), F32)
    return (jnp.concatenate([cos, cos, pad], axis=1), jnp.concatenate([-sin, sin, pad], axis=1))


def _swap_halves(w):
    half = w.shape[-1] // 2
    return jnp.concatenate([w[..., half:], w[..., :half]], axis=-1)


def _pad_lanes(w, width=LANES):
    return jnp.pad(w, [(0, 0)] * (w.ndim - 1) + [(0, width - w.shape[-1])])


def _token_tile(t):
    for tm in (320, 256, 128):
        if t % tm == 0:
            return tm
    raise ValueError(f"token count {t} is not a multiple of {LANES}")


def kernel(x_prompt, x_sample, cache_mla_ckv, cache_mla_krope, cache_fox_k, cache_fox_v, cache_fox_logf, cache_mem_k, cache_mem_v, state_conv, page_table, mem_prompt, norm_mix, w_in, conv_w, conv_b, conv_ln_g, conv_ln_b, mla_q_norm, mla_w_uq, mla_kv_norm, mla_w_uk, mla_w_uv, fox_f_bias, w_out, norm_xattn, norm_mem, xa_w_q, xa_w_kv, xa_w_o, norm_ffn, router_group_w, router_group_b, router_expert_w, router_expert_b, exp_w_gate, exp_w_up, exp_w_down, norm_final):
    batch, seq, d = x_prompt.shape
    n_dec, t_dec, _ = x_sample.shape
    assert t_dec == 1, "the sample group decodes one row per sequence"
    depth = w_in.shape[0]
    n_pages, page = page_table.shape[1], cache_mla_ckv.shape[2]
    assert n_pages % _PAGES_PER_STEP == 0
    tp = batch * seq
    t = tp + n_dec
    tm = _token_tile(t)
    t_att = min(512, seq)
    n_mem = mem_prompt.shape[1]
    e_xa = XA_HEADS * XA_HEAD_DIM

    pos = jnp.concatenate([jnp.tile(jnp.arange(seq), batch), jnp.full((n_dec,), n_pages * page)])
    cos_t, sin_t = _rope_tables(pos)

    sizes = (C_CONV, C_CONV, MLA_Q_RANK, MLA_KV_RANK, MLA_ROPE, FOX_HEADS * FOX_HEAD_DIM,
             FOX_HEAD_DIM, FOX_HEAD_DIM, FOX_HEADS)
    bounds = [0]
    for sz in sizes:
        bounds.append(bounds[-1] + sz)
    wa, wg_, wcq, wckv, wkr, wfq, wfk, wfv, wff = [w_in[:, :, bounds[k]:bounds[k + 1]] for k in range(9)]
    w_in_ext = jnp.concatenate([wa, wg_, wcq, wckv, wfq, wfk, wfv, _pad_lanes(wkr),
                                _pad_lanes(_swap_halves(wkr)), _pad_lanes(wff)], axis=-1).astype(BF16)
    uq_nope = mla_w_uq[..., :MLA_NOPE].reshape(depth, MLA_Q_RANK, MLA_HEADS * MLA_NOPE)
    uq_rope = mla_w_uq[..., MLA_NOPE:]
    w_uq_ext = jnp.concatenate(
        [uq_nope, _pad_lanes(uq_rope).reshape(depth, MLA_Q_RANK, MLA_HEADS * LANES),
         _pad_lanes(_swap_halves(uq_rope)).reshape(depth, MLA_Q_RANK, MLA_HEADS * LANES)], axis=-1).astype(BF16)
    w_ukt = jnp.transpose(mla_w_uk, (0, 2, 3, 1)).astype(BF16)
    w_uv = jnp.transpose(mla_w_uv, (0, 2, 1, 3)).astype(BF16)
    f_bias_row = _pad_lanes(fox_f_bias)[:, None, :]
    w_out_b = w_out.astype(BF16)
    w_q_b = xa_w_q.astype(BF16)
    w_kv_b = xa_w_kv.astype(BF16)
    w_o_b = xa_w_o.astype(BF16)
    w_r = _pad_lanes(jnp.concatenate([router_group_w, router_expert_w], axis=-1), ROUTE_LANES)
    w_r_hi = w_r.astype(BF16)
    w_r_lo = (w_r - w_r_hi.astype(F32)).astype(BF16)
    b_r = _pad_lanes(jnp.concatenate([router_group_b, router_expert_b], axis=-1), ROUTE_LANES)[:, None, :]
    w_gate_b = exp_w_gate.astype(BF16)
    w_up_b = exp_w_up.astype(BF16)
    w_down_b = exp_w_down.astype(BF16)
    row = lambda v: v[:, None, :]
    norm_mix_r, norm_xattn_r, norm_ffn_r, norm_mem_r = row(norm_mix), row(norm_xattn), row(norm_ffn), row(norm_mem)
    q_norm_r, kv_norm_r = row(mla_q_norm), row(mla_kv_norm)
    conv_b_r, ln_g_r, ln_b_r = row(conv_b), row(conv_ln_g), row(conv_ln_b)
    lf_pool_t = jnp.pad(jnp.transpose(cache_fox_logf, (0, 1, 3, 2)),
                        ((0, 0), (0, 0), (0, HEADS_PAD - FOX_HEADS), (0, 0)))
    mem_k = cache_mem_k.reshape(depth, n_dec, n_mem, e_xa)
    mem_v = cache_mem_v.reshape(depth, n_dec, n_mem, e_xa)

    mk, mv, mk_b, mv_b = _memkv(mem_prompt.reshape(batch * n_mem, d), norm_mem_r, w_kv_b, n_mem)
    mk_b = mk_b.reshape(depth, batch, n_mem, e_xa)
    mv_b = mv_b.reshape(depth, batch, n_mem, e_xa)

    x = jnp.concatenate([x_prompt.reshape(tp, d), x_sample.reshape(n_dec, d)], axis=0)
    ypair = None
    leaves = {k: [] for k in ("ckv", "kr", "fk", "fv", "logf", "conv_p", "conv_s")}
    for l in range(depth):
        (x, u, ckv, kr, fk, fv, logf, kcat, qcat, fq_b, fk_b, fv_b) = _proj(
            x, ypair, norm_mix_r[l], w_in_ext[l], w_uq_ext[l], w_ukt[l], q_norm_r[l], kv_norm_r[l],
            f_bias_row[l], cos_t, sin_t, tm)
        for name, val in (("ckv", ckv), ("kr", kr), ("fk", fk), ("fv", fv), ("logf", logf)):
            leaves[name].append(val)
        u_p = u[:tp].reshape(batch, seq, C_CONV)
        u_s = u[tp:]
        leaves["conv_p"].append(u_p[:, seq - (CONV_WIDTH - 1):])
        leaves["conv_s"].append(jnp.concatenate([state_conv[l][:, 1:], u_s[:, None, :]], axis=1))
        conv_p = _conv_prompt(u, batch, seq, conv_w[l], conv_b_r[l], ln_g_r[l], ln_b_r[l], min(256, seq))
        conv_s = _conv_sample(state_conv[l], u_s, conv_w[l], conv_b_r[l], ln_g_r[l], ln_b_r[l], min(32, n_dec))
        lat_p = _flash(qcat, kcat, kcat, None, None, batch, seq, MLA_HEADS, MLA_QK, MLA_KV_RANK, t_att)
        lf_t = jnp.pad(jnp.transpose(logf[:tp].reshape(batch, seq, FOX_HEADS), (0, 2, 1)),
                       ((0, 0), (0, HEADS_PAD - FOX_HEADS), (0, 0)))
        cum_row = _cumsum_time(lf_t)
        cum_col = jnp.transpose(cum_row, (0, 2, 1)).reshape(tp, HEADS_PAD)
        fox_p = _flash(fq_b, fk_b, fv_b, cum_col, cum_row, batch, seq, FOX_HEADS, FOX_HEAD_DIM, FOX_HEAD_DIM, t_att)
        pad_heads = lambda a: jnp.pad(a, ((0, 0), (0, HEADS_PAD - a.shape[1]), (0, 0)))
        q_s = pad_heads(qcat[tp:].reshape(n_dec, MLA_HEADS, MLA_QK))
        fq_s = pad_heads(fq_b[tp:].reshape(n_dec, FOX_HEADS, FOX_HEAD_DIM))
        cn = pad_heads(logf[tp:][:, :, None])
        lat_s, fox_s = _decode(l, page_table, q_s, fq_s, kcat[tp:][:, None, :], fk_b[tp:][:, None, :],
                               fv_b[tp:][:, None, :], cn, cache_mla_ckv, cache_mla_krope, cache_fox_k,
                               cache_fox_v, lf_pool_t)
        lat_s = lat_s[:, :MLA_HEADS].reshape(n_dec, MLA_HEADS * MLA_KV_RANK)
        fox_s = fox_s[:, :FOX_HEADS].reshape(n_dec, FOX_HEADS * FOX_HEAD_DIM)
        x, qx = _merge(x, jnp.concatenate([conv_p, conv_s]), jnp.concatenate([lat_p, lat_s]),
                       jnp.concatenate([fox_p, fox_s]), w_uv[l], w_out_b[l], norm_xattn_r[l], w_q_b[l], tm)
        o_p = _xattn_prompt(l, qx, mk_b, mv_b, batch, seq, min(512, seq))
        o_s = _xattn_sample(l, qx[tp:][:, None, :], mem_k, mem_v, min(8, n_dec))
        x, xn, route = _route(x, jnp.concatenate([o_p, o_s.reshape(n_dec, e_xa)]), w_o_b[l], norm_ffn_r[l],
                              w_r_hi[l], w_r_lo[l], b_r[l], tm)
        tok, dst, tile_e, n_tiles, gate_rows = _moe_plan(route, t)
        ypair = _moe(xn, tok, dst, tile_e, n_tiles, gate_rows, w_gate_b[l], w_up_b[l], w_down_b[l])
    y = _final(x, ypair, norm_final[None, :], tm)

    def stack_p(name, width):
        return jnp.stack([v[:tp].reshape(batch, seq, width) for v in leaves[name]])

    def stack_s(name, width):
        return jnp.stack([v[tp:].reshape(n_dec, 1, width) for v in leaves[name]])

    mem_shape = (depth, batch, n_mem, XA_HEADS, XA_HEAD_DIM)
    return (y[:tp].reshape(batch, seq, d), y[tp:].reshape(n_dec, 1, d),
            stack_p("ckv", MLA_KV_RANK), stack_p("kr", MLA_ROPE), stack_p("fk", FOX_HEAD_DIM),
            stack_p("fv", FOX_HEAD_DIM), stack_p("logf", FOX_HEADS),
            mk.reshape(mem_shape), mv.reshape(mem_shape), jnp.stack(leaves["conv_p"]),
            stack_s("ckv", MLA_KV_RANK), stack_s("kr", MLA_ROPE), stack_s("fk", FOX_HEAD_DIM),
            stack_s("fv", FOX_HEAD_DIM), stack_s("logf", FOX_HEADS), jnp.stack(leaves["conv_s"]))
```

```python
import functools

import jax
import jax.numpy as jnp
from jax import lax
from jax.experimental import pallas as pl
from jax.experimental.pallas import tpu as pltpu

F32 = jnp.float32
BF16 = jnp.bfloat16

EPS = 1e-6
ROPE_THETA = 10000.0
C_CONV = 512
CONV_WIDTH = 31
MLA_HEADS = 6
MLA_NOPE = 128
MLA_ROPE = 64
MLA_Q_RANK = 512
MLA_KV_RANK = 256
FOX_HEADS = 6
FOX_HEAD_DIM = 128
XA_HEADS = 4
XA_HEAD_DIM = 128
N_GROUPS = 4
EXPERTS_PER_GROUP = 4
N_EXPERTS = 16
D_EXPERT = 512
MLA_SCALE = (MLA_NOPE + MLA_ROPE) ** -0.5
FOX_SCALE = FOX_HEAD_DIM ** -0.5
XA_SCALE = XA_HEAD_DIM ** -0.5

LANES = 128
SUBLANES = 8
VMEM_LIMIT_BYTES = 56 * 1024 * 1024

HEADS_PAD = SUBLANES
MLA_QK = 3 * LANES

_SEG_A = 0
_SEG_G = _SEG_A + C_CONV
_SEG_CQ = _SEG_G + C_CONV
_SEG_CKV = _SEG_CQ + MLA_Q_RANK
_SEG_FQ = _SEG_CKV + MLA_KV_RANK
_SEG_FK = _SEG_FQ + FOX_HEADS * FOX_HEAD_DIM
_SEG_FV = _SEG_FK + FOX_HEAD_DIM
_SEG_KR = _SEG_FV + FOX_HEAD_DIM
_SEG_KRS = _SEG_KR + LANES
_SEG_FF = _SEG_KRS + LANES
D_IN_EXT = _SEG_FF + LANES

_UQ_NOPE = 0
_UQ_ROPE = MLA_HEADS * MLA_NOPE
_UQ_ROPES = _UQ_ROPE + MLA_HEADS * LANES
D_UQ_EXT = _UQ_ROPES + MLA_HEADS * LANES

ROUTE_LANES = LANES


def _params(*sem, row_dma=False):
    return pltpu.CompilerParams(dimension_semantics=sem, vmem_limit_bytes=VMEM_LIMIT_BYTES,
                                disable_bounds_checks=row_dma)


def _const_spec(shape):
    nd = len(shape)
    return pl.BlockSpec(shape, lambda *_: (0,) * nd, pipeline_mode=pl.Buffered(1))


def _rms(x, g):
    return x * lax.rsqrt(jnp.mean(x * x, axis=-1, keepdims=True) + EPS) * g


def _dot(a, b):
    return jnp.dot(a, b, preferred_element_type=F32)


def _dot_nt(a, b):
    return lax.dot_general(a, b, (((1,), (1,)), ((), ())), preferred_element_type=F32)


def _sigmoid(x):
    return 1.0 / (1.0 + jnp.exp(-x))


def _silu(x):
    return x * _sigmoid(x)


def _log_sigmoid(z):
    return jnp.minimum(z, 0.0) - jnp.log(1.0 + jnp.exp(-jnp.abs(z)))


def _split3(x):
    hi = x.astype(BF16)
    r1 = x - hi.astype(F32)
    mid = r1.astype(BF16)
    lo = (r1 - mid.astype(F32)).astype(BF16)
    return hi, mid, lo


def _proj_kernel(has_y, *refs):
    if has_y:
        x_ref, y0_ref, y1_ref = refs[:3]
        refs = refs[3:]
    else:
        x_ref = refs[0]
        refs = refs[1:]
    (g_ref, win_ref, wuq_ref, wukt_ref, qn_ref, kvn_ref, fb_ref, cos_ref, sin_ref,
     xo_ref, u_ref, ckv_ref, kr_ref, fk_ref, fv_ref, logf_ref,
     kcat_ref, qcat_ref, fqb_ref, fkb_ref, fvb_ref) = refs

    x = x_ref[...]
    if has_y:
        x = x + y0_ref[...] + y1_ref[...]
    xo_ref[...] = x
    xb = _rms(x, g_ref[...]).astype(BF16)

    def seg(lo, width):
        return _dot(xb, win_ref[:, lo:lo + width])

    u_ref[...] = seg(_SEG_A, C_CONV) * _sigmoid(seg(_SEG_G, C_CONV))
    ckv = _rms(seg(_SEG_CKV, MLA_KV_RANK), kvn_ref[...])
    ckv_ref[...] = ckv
    cos = cos_ref[...]
    sin = sin_ref[...]
    kr = seg(_SEG_KR, LANES) * cos + seg(_SEG_KRS, LANES) * sin
    kr_ref[...] = kr[:, :MLA_ROPE]
    kcat_ref[:, :MLA_KV_RANK] = ckv.astype(BF16)
    kcat_ref[:, MLA_KV_RANK:] = kr.astype(BF16)
    cqn = _rms(seg(_SEG_CQ, MLA_Q_RANK), qn_ref[...]).astype(BF16)
    q = _dot(cqn, wuq_ref[...])
    for h in range(MLA_HEADS):
        qn = q[:, _UQ_NOPE + h * MLA_NOPE:_UQ_NOPE + (h + 1) * MLA_NOPE].astype(BF16)
        ql = _dot(qn, wukt_ref[h]) * MLA_SCALE
        qr = (q[:, _UQ_ROPE + h * LANES:_UQ_ROPE + (h + 1) * LANES] * cos
              + q[:, _UQ_ROPES + h * LANES:_UQ_ROPES + (h + 1) * LANES] * sin) * MLA_SCALE
        qcat_ref[:, h * MLA_QK:h * MLA_QK + MLA_KV_RANK] = ql.astype(BF16)
        qcat_ref[:, h * MLA_QK + MLA_KV_RANK:(h + 1) * MLA_QK] = qr.astype(BF16)
    fqb_ref[...] = (seg(_SEG_FQ, FOX_HEADS * FOX_HEAD_DIM) * FOX_SCALE).astype(BF16)
    fk = seg(_SEG_FK, FOX_HEAD_DIM)
    fv = seg(_SEG_FV, FOX_HEAD_DIM)
    fk_ref[...] = fk
    fv_ref[...] = fv
    fkb_ref[...] = fk.astype(BF16)
    fvb_ref[...] = fv.astype(BF16)
    logf = _log_sigmoid(seg(_SEG_FF, LANES) + fb_ref[...])
    logf_ref[...] = logf[:, :FOX_HEADS]


def _proj(x, ypair, g, w_in_ext, w_uq_ext, w_ukt, q_norm, kv_norm, f_bias_row, cos_t, sin_t, tm):
    t, d = x.shape
    nblk = t // tm
    has_y = ypair is not None
    row = lambda w: pl.BlockSpec((tm, w), lambda i: (i, 0))
    in_specs = [row(d)]
    args = [x]
    if has_y:
        in_specs += [row(d), pl.BlockSpec((tm, d), lambda i: (i + nblk, 0))]
        args += [ypair, ypair]
    in_specs += [_const_spec((1, d)), _const_spec(w_in_ext.shape), _const_spec(w_uq_ext.shape),
                 _const_spec(w_ukt.shape), _const_spec((1, MLA_Q_RANK)), _const_spec((1, MLA_KV_RANK)),
                 _const_spec((1, LANES)), row(LANES), row(LANES)]
    args += [g, w_in_ext, w_uq_ext, w_ukt, q_norm, kv_norm, f_bias_row, cos_t, sin_t]
    widths = [(d, F32), (C_CONV, F32), (MLA_KV_RANK, F32), (MLA_ROPE, F32), (FOX_HEAD_DIM, F32),
              (FOX_HEAD_DIM, F32), (FOX_HEADS, F32), (MLA_QK, BF16), (MLA_HEADS * MLA_QK, BF16),
              (FOX_HEADS * FOX_HEAD_DIM, BF16), (FOX_HEAD_DIM, BF16), (FOX_HEAD_DIM, BF16)]
    return pl.pallas_call(
        functools.partial(_proj_kernel, has_y),
        grid=(nblk,),
        in_specs=in_specs,
        out_specs=[row(w) for w, _ in widths],
        out_shape=[jax.ShapeDtypeStruct((t, w), dt) for w, dt in widths],
        compiler_params=_params("parallel"),
        name="proj",
    )(*args)


def _cumsum_kernel(lf_ref, tri_ref, o_ref):
    n_chunks = lf_ref.shape[2] // LANES
    tri = tri_ref[...]
    carry = jnp.zeros((HEADS_PAD, 1), F32)
    for c in range(n_chunks):
        hi, mid, lo = _split3(lf_ref[0, :, c * LANES:(c + 1) * LANES])
        local = _dot(hi, tri) + _dot(mid, tri) + _dot(lo, tri)
        cum = local + carry
        o_ref[0, :, c * LANES:(c + 1) * LANES] = cum
        carry = cum[:, LANES - 1:LANES]


def _cumsum_time(lf_t):
    b, hp, s = lf_t.shape
    tri = (jnp.arange(LANES)[:, None] <= jnp.arange(LANES)[None, :]).astype(BF16)
    return pl.pallas_call(
        _cumsum_kernel,
        grid=(b,),
        in_specs=[pl.BlockSpec((1, hp, s), lambda i: (i, 0, 0)), _const_spec((LANES, LANES))],
        out_specs=pl.BlockSpec((1, hp, s), lambda i: (i, 0, 0)),
        out_shape=jax.ShapeDtypeStruct((b, hp, s), F32),
        compiler_params=_params("parallel"),
        name="cumsum",
    )(lf_t, tri)


_CONV_HALO = 32
_CONV_ROWS = 64


def _layer_norm_silu(y, g, b):
    mu = jnp.mean(y, axis=-1, keepdims=True)
    yc = y - mu
    var = jnp.mean(yc * yc, axis=-1, keepdims=True)
    return _silu(yc * lax.rsqrt(var + EPS) * g + b)


def _conv_prompt_kernel(blocks_per_seq, u_ref, halo_ref, w_ref, b_ref, g_ref, be_ref, o_ref, ext_ref):
    tm = u_ref.shape[0]
    first = (pl.program_id(0) % blocks_per_seq) == 0
    ext_ref[:_CONV_HALO, :] = jnp.where(first, 0.0, halo_ref[...])
    ext_ref[_CONV_HALO:, :] = u_ref[...]
    off = _CONV_HALO - (CONV_WIDTH - 1)
    for r0 in range(0, tm, _CONV_ROWS):
        acc = jnp.zeros((_CONV_ROWS, C_CONV), F32) + b_ref[...]
        for k in range(CONV_WIDTH):
            acc = acc + ext_ref[r0 + k + off:r0 + k + off + _CONV_ROWS, :] * w_ref[k:k + 1, :]
        o_ref[r0:r0 + _CONV_ROWS, :] = _layer_norm_silu(acc, g_ref[...], be_ref[...]).astype(o_ref.dtype)


def _conv_prompt(u, batch, seq, conv_w, conv_b, ln_g, ln_b, tm):
    n_rows = batch * seq
    nblk = n_rows // tm
    per_halo = tm // _CONV_HALO
    return pl.pallas_call(
        functools.partial(_conv_prompt_kernel, seq // tm),
        grid=(nblk,),
        in_specs=[pl.BlockSpec((tm, C_CONV), lambda i: (i, 0)),
                  pl.BlockSpec((_CONV_HALO, C_CONV), lambda i: (jnp.maximum(i * per_halo - 1, 0), 0)),
                  _const_spec((CONV_WIDTH, C_CONV)), _const_spec((1, C_CONV)),
                  _const_spec((1, C_CONV)), _const_spec((1, C_CONV))],
        out_specs=pl.BlockSpec((tm, C_CONV), lambda i: (i, 0)),
        out_shape=jax.ShapeDtypeStruct((n_rows, C_CONV), BF16),
        scratch_shapes=[pltpu.VMEM((tm + _CONV_HALO, C_CONV), F32)],
        compiler_params=_params("parallel"),
        name="conv_prompt",
    )(u, u, conv_w, conv_b, ln_g, ln_b)


def _conv_sample_kernel(st_ref, u_ref, w_ref, b_ref, g_ref, be_ref, o_ref):
    w = w_ref[...]
    hist = jnp.sum(st_ref[...] * w[None, :CONV_WIDTH - 1, :], axis=1)
    y = hist + u_ref[...] * w[CONV_WIDTH - 1:CONV_WIDTH, :] + b_ref[...]
    o_ref[...] = _layer_norm_silu(y, g_ref[...], be_ref[...]).astype(o_ref.dtype)


def _conv_sample(state, u_s, conv_w, conv_b, ln_g, ln_b, nb):
    n = state.shape[0]
    return pl.pallas_call(
        _conv_sample_kernel,
        grid=(n // nb,),
        in_specs=[pl.BlockSpec((nb, CONV_WIDTH - 1, C_CONV), lambda i: (i, 0, 0)),
                  pl.BlockSpec((nb, C_CONV), lambda i: (i, 0)),
                  _const_spec((CONV_WIDTH, C_CONV)), _const_spec((1, C_CONV)),
                  _const_spec((1, C_CONV)), _const_spec((1, C_CONV))],
        out_specs=pl.BlockSpec((nb, C_CONV), lambda i: (i, 0)),
        out_shape=jax.ShapeDtypeStruct((n, C_CONV), BF16),
        compiler_params=_params("parallel"),
        name="conv_sample",
    )(state, u_s, conv_w, conv_b, ln_g, ln_b)


def _flash_kernel(n_heads, dq, dv, has_bias, *refs):
    if has_bias:
        q_ref, k_ref, v_ref, ci_ref, cs_ref, o_ref, m_ref, l_ref, acc_ref, ci_rep_ref = refs
    else:
        q_ref, k_ref, v_ref, o_ref, m_ref, l_ref, acc_ref = refs
    i = pl.program_id(1)
    j = pl.program_id(2)
    tq = q_ref.shape[0]
    tk = k_ref.shape[0]

    def lanes(x, width):
        return jnp.concatenate([x] * (width // LANES), axis=1)

    @pl.when(j == 0)
    def _():
        m_ref[...] = jnp.full(m_ref.shape, -jnp.inf, F32)
        l_ref[...] = jnp.zeros(l_ref.shape, F32)
        acc_ref[...] = jnp.zeros(acc_ref.shape, F32)
        if has_bias:
            for h in range(n_heads):
                ci_rep_ref[h] = jnp.broadcast_to(ci_ref[:, h:h + 1], (tq, LANES))

    def update(masked):
        k = k_ref[...]
        v = v_ref[...]
        if masked:
            keep = (lax.broadcasted_iota(jnp.int32, (tq, tk), 1)
                    <= lax.broadcasted_iota(jnp.int32, (tq, tk), 0))
        for h in range(n_heads):
            s = _dot_nt(q_ref[:, h * dq:(h + 1) * dq], k)
            if has_bias:
                s = s + (lanes(ci_rep_ref[h], tk) - cs_ref[0, h:h + 1, :])
            if masked:
                s = jnp.where(keep, s, -jnp.inf)
            m_prev = m_ref[h]
            m_new = jnp.maximum(m_prev, jnp.max(s, axis=-1, keepdims=True))
            alpha = jnp.exp(m_prev - m_new)
            p = jnp.exp(s - lanes(m_new, tk))
            l_ref[h] = alpha * l_ref[h] + jnp.sum(p, axis=-1, keepdims=True)
            acc_ref[h] = lanes(alpha, dv) * acc_ref[h] + _dot(p.astype(BF16), v)
            m_ref[h] = m_new

    @pl.when(j < i)
    def _():
        update(False)

    @pl.when(j == i)
    def _():
        update(True)
        for h in range(n_heads):
            o_ref[:, h * dv:(h + 1) * dv] = (acc_ref[h] / lanes(l_ref[h], dv)).astype(o_ref.dtype)


def _flash(q, k, v, ci, cs, batch, seq, n_heads, dq, dv, tq):
    nq = seq // tq
    has_bias = ci is not None
    in_specs = [pl.BlockSpec((tq, n_heads * dq), lambda b, i, j: (b * nq + i, 0)),
                pl.BlockSpec((tq, dq), lambda b, i, j: (b * nq + jnp.minimum(i, j), 0)),
                pl.BlockSpec((tq, dv), lambda b, i, j: (b * nq + jnp.minimum(i, j), 0))]
    args = [q, k, v]
    if has_bias:
        in_specs += [pl.BlockSpec((tq, HEADS_PAD), lambda b, i, j: (b * nq + i, 0)),
                     pl.BlockSpec((1, HEADS_PAD, tq), lambda b, i, j: (b, 0, jnp.minimum(i, j)))]
        args += [ci, cs]
    return pl.pallas_call(
        functools.partial(_flash_kernel, n_heads, dq, dv, has_bias),
        grid=(batch, nq, nq),
        in_specs=in_specs,
        out_specs=pl.BlockSpec((tq, n_heads * dv), lambda b, i, j: (b * nq + i, 0)),
        out_shape=jax.ShapeDtypeStruct((batch * seq, n_heads * dv), BF16),
        scratch_shapes=[pltpu.VMEM((n_heads, tq, LANES), F32), pltpu.VMEM((n_heads, tq, LANES), F32),
                        pltpu.VMEM((n_heads, tq, dv), F32)]
        + ([pltpu.VMEM((n_heads, tq, LANES), F32)] if has_bias else []),
        compiler_params=_params("parallel", "parallel", "arbitrary"),
        name="flash_bias" if has_bias else "flash",
    )(*args)


_PAGES_PER_STEP = 32


def _decode_kernel(layer, n_pg, n_pages, pt_ref, qlat_t_ref, qrope_t_ref, fq_t_ref, q_ref, fq_ref,
                   knew_ref, fknew_ref, fvnew_ref, cn_ref,
                   ckv_hbm, kr_hbm, fk_hbm, fv_hbm, lf_hbm,
                   ml_ref, fo_ref,
                   ckv_buf, kr_buf, fk_buf, fv_buf, lf_buf, lfpad_ref, sem,
                   m1_ref, l1_ref, a1_ref, m2_ref, l2_ref, a2_ref, carry_ref):
    step = pl.program_id(0)
    n_steps = pl.num_programs(0)
    n_chunks = n_pages // n_pg
    c = step % n_chunks
    slot = step % 2
    page = ckv_buf.shape[2]
    rows = n_pg * page
    pools = ((ckv_hbm, ckv_buf), (kr_hbm, kr_buf), (fk_hbm, fk_buf), (fv_hbm, fv_buf), (lf_hbm, lf_buf))

    def start_fetch(target, into):
        base = (target // n_chunks) * n_pages + (n_chunks - 1 - target % n_chunks) * n_pg
        for g in range(n_pg):
            pg = pt_ref[base + g]
            for hbm, buf in pools:
                pltpu.make_async_copy(hbm.at[layer, pg], buf.at[into, g], sem.at[into]).start()

    def wait_fetch(into):
        for hbm, buf in pools:
            pltpu.make_async_copy(hbm.at[layer, pl.ds(0, n_pg)], buf.at[into], sem.at[into]).wait()

    @pl.when(step == 0)
    def _():
        start_fetch(0, 0)
        lfpad_ref[...] = jnp.zeros(lfpad_ref.shape, F32)

    @pl.when(c == 0)
    def _():
        m1_ref[...] = jnp.full(m1_ref.shape, -jnp.inf, F32)
        m2_ref[...] = jnp.full(m2_ref.shape, -jnp.inf, F32)
        l1_ref[...] = jnp.zeros(l1_ref.shape, F32)
        l2_ref[...] = jnp.zeros(l2_ref.shape, F32)
        a1_ref[...] = jnp.zeros(a1_ref.shape, F32)
        a2_ref[...] = jnp.zeros(a2_ref.shape, F32)
        carry_ref[...] = jnp.zeros(carry_ref.shape, F32)

    wait_fetch(slot)
    start_fetch(jnp.where(step + 1 < n_steps, step + 1, 0), 1 - slot)

    q = q_ref[0]
    fq = fq_ref[0]
    cn = cn_ref[0]

    def heads_first(st):
        return jnp.transpose(st)[:HEADS_PAD]

    def online(s, v, m_ref, l_ref, a_ref):
        m_prev = m_ref[...]
        m_new = jnp.maximum(m_prev, jnp.max(s, axis=-1, keepdims=True))
        alpha = jnp.exp(m_prev - m_new)
        p = jnp.exp(s - m_new)
        l_ref[...] = alpha * l_ref[...] + jnp.sum(p, axis=-1, keepdims=True)
        a_ref[...] = alpha * a_ref[...] + _dot(p.astype(BF16), v)
        m_ref[...] = m_new

    lf_pages = []
    for g in range(n_pg):
        lfpad_ref[g, :, :FOX_HEADS] = lf_buf[slot, g]
        lf_pages.append(jnp.transpose(lfpad_ref[g])[:HEADS_PAD])
    lf = jnp.concatenate(lf_pages, axis=0)
    lane = lax.broadcasted_iota(jnp.int32, lf.shape, 1)
    incl = lf
    d = 1
    while d < page:
        incl = incl + jnp.where(lane + d < page, pltpu.roll(incl, page - d, 1), 0.0)
        d *= 2
    excl = incl - lf
    carry = carry_ref[...][:, :1]
    bias = [None] * n_pg
    for g in reversed(range(n_pg)):
        bias[g] = excl[g * HEADS_PAD:(g + 1) * HEADS_PAD] + (carry + cn)
        carry = carry + incl[g * HEADS_PAD:(g + 1) * HEADS_PAD, :1]
    carry_ref[...] = jnp.broadcast_to(carry, carry_ref.shape)

    kc = ckv_buf[slot].reshape(rows, MLA_KV_RANK).astype(BF16)
    krc = kr_buf[slot].reshape(rows, MLA_ROPE).astype(BF16)
    fkc = fk_buf[slot].reshape(rows, FOX_HEAD_DIM).astype(BF16)
    s1 = heads_first(_dot(kc, qlat_t_ref[0]) + _dot(krc, qrope_t_ref[0]))
    s2 = heads_first(_dot(fkc, fq_t_ref[0])) + jnp.concatenate(bias, axis=1)
    online(s1, kc, m1_ref, l1_ref, a1_ref)
    online(s2, fv_buf[slot].reshape(rows, FOX_HEAD_DIM).astype(BF16), m2_ref, l2_ref, a2_ref)

    @pl.when(step == n_steps - 1)
    def _():
        wait_fetch(1 - slot)

    @pl.when(c == n_chunks - 1)
    def _():
        def finish(s_new, v_new, m_ref, l_ref, a_ref, o_ref):
            m_prev = m_ref[...]
            m_new = jnp.maximum(m_prev, s_new)
            alpha = jnp.exp(m_prev - m_new)
            p = jnp.exp(s_new - m_new)
            l = alpha * l_ref[...] + p
            o_ref[0] = ((alpha * a_ref[...] + p * v_new) / l).astype(o_ref.dtype)

        k_new = knew_ref[0].astype(F32)
        s_new = jnp.sum(q.astype(F32) * k_new, axis=-1, keepdims=True)
        finish(s_new, k_new[:, :MLA_KV_RANK], m1_ref, l1_ref, a1_ref, ml_ref)
        s_new = jnp.sum(fq.astype(F32) * fknew_ref[0].astype(F32), axis=-1, keepdims=True)
        finish(s_new, fvnew_ref[0].astype(F32), m2_ref, l2_ref, a2_ref, fo_ref)


def _decode(layer, page_table, q_s, fq_s, k_new, fk_new, fv_new, cn, ckv_pool, kr_pool, fk_pool, fv_pool, lf_pool):
    n, n_pages = page_table.shape
    page = ckv_pool.shape[2]
    n_pg = _PAGES_PER_STEP
    n_chunks = n_pages // n_pg
    pt_flat = page_table.reshape(-1)
    q_t = _pad_lanes(jnp.transpose(q_s, (0, 2, 1)))
    qlat_t = q_t[:, :MLA_KV_RANK]
    qrope_t = q_t[:, MLA_KV_RANK:MLA_KV_RANK + MLA_ROPE]
    fq_t = _pad_lanes(jnp.transpose(fq_s, (0, 2, 1)))

    def per_sample(shape):
        return pl.BlockSpec((1,) + shape, lambda s, pt: (s // n_chunks, 0, 0))

    in_specs = [per_sample((MLA_KV_RANK, LANES)), per_sample((MLA_ROPE, LANES)), per_sample((FOX_HEAD_DIM, LANES)),
                per_sample((HEADS_PAD, MLA_QK)), per_sample((HEADS_PAD, FOX_HEAD_DIM)),
                per_sample((1, MLA_QK)), per_sample((1, FOX_HEAD_DIM)), per_sample((1, FOX_HEAD_DIM)),
                per_sample((HEADS_PAD, 1))] + [pl.BlockSpec(memory_space=pl.ANY)] * 5
    buf = lambda width: pltpu.VMEM((2, n_pg, page, width), F32)
    grid_spec = pltpu.PrefetchScalarGridSpec(
        num_scalar_prefetch=1,
        grid=(n * n_chunks,),
        in_specs=in_specs,
        out_specs=[per_sample((HEADS_PAD, MLA_KV_RANK)), per_sample((HEADS_PAD, FOX_HEAD_DIM))],
        scratch_shapes=[buf(MLA_KV_RANK), buf(MLA_ROPE), buf(FOX_HEAD_DIM), buf(FOX_HEAD_DIM), buf(FOX_HEADS),
                        pltpu.VMEM((n_pg, page, LANES), F32), pltpu.SemaphoreType.DMA((2,)),
                        pltpu.VMEM((HEADS_PAD, 1), F32), pltpu.VMEM((HEADS_PAD, 1), F32),
                        pltpu.VMEM((HEADS_PAD, MLA_KV_RANK), F32),
                        pltpu.VMEM((HEADS_PAD, 1), F32), pltpu.VMEM((HEADS_PAD, 1), F32),
                        pltpu.VMEM((HEADS_PAD, FOX_HEAD_DIM), F32),
                        pltpu.VMEM((HEADS_PAD, LANES), F32)])
    return pl.pallas_call(
        functools.partial(_decode_kernel, layer, n_pg, n_pages),
        grid_spec=grid_spec,
        out_shape=[jax.ShapeDtypeStruct((n, HEADS_PAD, MLA_KV_RANK), BF16),
                   jax.ShapeDtypeStruct((n, HEADS_PAD, FOX_HEAD_DIM), BF16)],
        compiler_params=_params("arbitrary", row_dma=True),
        name="decode",
    )(pt_flat, qlat_t, qrope_t, fq_t, q_s, fq_s, k_new, fk_new, fv_new, cn,
      ckv_pool, kr_pool, fk_pool, fv_pool, lf_pool)


def _merge_kernel(x_ref, conv_ref, lat_ref, fox_ref, wuv_ref, wout_ref, g_ref, wq_ref,
                  xo_ref, qx_ref, cat_ref):
    cat_ref[:, :C_CONV] = conv_ref[...]
    for h in range(MLA_HEADS):
        o = _dot(lat_ref[:, h * MLA_KV_RANK:(h + 1) * MLA_KV_RANK], wuv_ref[h])
        cat_ref[:, C_CONV + h * MLA_NOPE:C_CONV + (h + 1) * MLA_NOPE] = o.astype(BF16)
    base = C_CONV + MLA_HEADS * MLA_NOPE
    cat_ref[:, base:] = fox_ref[...]
    x = x_ref[...] + _dot(cat_ref[...], wout_ref[...])
    xo_ref[...] = x
    xb = _rms(x, g_ref[...]).astype(BF16)
    qx_ref[...] = (_dot(xb, wq_ref[...]) * XA_SCALE).astype(BF16)


def _merge(x, conv_o, lat, fox_o, w_uv, w_out, g, w_q, tm):
    t, d = x.shape
    row = lambda w: pl.BlockSpec((tm, w), lambda i: (i, 0))
    dq = w_q.shape[1]
    return pl.pallas_call(
        _merge_kernel,
        grid=(t // tm,),
        in_specs=[row(d), row(conv_o.shape[1]), row(lat.shape[1]), row(fox_o.shape[1]),
                  _const_spec(w_uv.shape), _const_spec(w_out.shape), _const_spec((1, d)), _const_spec(w_q.shape)],
        out_specs=[row(d), row(dq)],
        out_shape=[jax.ShapeDtypeStruct((t, d), F32), jax.ShapeDtypeStruct((t, dq), BF16)],
        scratch_shapes=[pltpu.VMEM((tm, w_out.shape[0]), BF16)],
        compiler_params=_params("parallel"),
        name="merge",
    )(x, conv_o, lat, fox_o, w_uv, w_out, g, w_q)


def _memkv_kernel(mem_ref, g_ref, w_ref, k_ref, v_ref, kb_ref, vb_ref):
    kv = _dot(_rms(mem_ref[...], g_ref[0]).astype(BF16), w_ref[0])
    half = kv.shape[1] // 2
    k_ref[0] = kv[:, :half]
    v_ref[0] = kv[:, half:]
    kb_ref[0] = kv[:, :half].astype(BF16)
    vb_ref[0] = kv[:, half:].astype(BF16)


def _memkv(mem, norm_mem, w_kv, tm):
    m, d = mem.shape
    depth, _, two_e = w_kv.shape
    e = two_e // 2
    out = pl.BlockSpec((1, tm, e), lambda l, i: (l, i, 0))
    return pl.pallas_call(
        _memkv_kernel,
        grid=(depth, m // tm),
        in_specs=[pl.BlockSpec((tm, d), lambda l, i: (i, 0)),
                  pl.BlockSpec((1, 1, d), lambda l, i: (l, 0, 0)),
                  pl.BlockSpec((1, d, two_e), lambda l, i: (l, 0, 0))],
        out_specs=[out, out, out, out],
        out_shape=[jax.ShapeDtypeStruct((depth, m, e), F32), jax.ShapeDtypeStruct((depth, m, e), F32),
                   jax.ShapeDtypeStruct((depth, m, e), BF16), jax.ShapeDtypeStruct((depth, m, e), BF16)],
        compiler_params=_params("parallel", "parallel"),
        name="memkv",
    )(mem, norm_mem, w_kv)


def _xattn_prompt_kernel(q_ref, k_ref, v_ref, o_ref):
    for h in range(XA_HEADS):
        sl = slice(h * XA_HEAD_DIM, (h + 1) * XA_HEAD_DIM)
        s = _dot_nt(q_ref[:, sl], k_ref[0, 0, :, sl])
        p = jnp.exp(s - jnp.max(s, axis=-1, keepdims=True))
        p = p / jnp.sum(p, axis=-1, keepdims=True)
        o_ref[:, sl] = _dot(p.astype(BF16), v_ref[0, 0, :, sl]).astype(o_ref.dtype)


def _xattn_prompt(layer, qx, mk_b, mv_b, batch, seq, tm):
    e = qx.shape[1]
    n_mem = mk_b.shape[2]
    per = seq // tm
    kv_spec = pl.BlockSpec((1, 1, n_mem, e), lambda i: (layer, i // per, 0, 0))
    return pl.pallas_call(
        _xattn_prompt_kernel,
        grid=(batch * per,),
        in_specs=[pl.BlockSpec((tm, e), lambda i: (i, 0)), kv_spec, kv_spec],
        out_specs=pl.BlockSpec((tm, e), lambda i: (i, 0)),
        out_shape=jax.ShapeDtypeStruct((batch * seq, e), BF16),
        compiler_params=_params("parallel"),
        name="xattn_prompt",
    )(qx, mk_b, mv_b)


def _xattn_sample_kernel(q_ref, k_ref, v_ref, o_ref):
    q = q_ref[...].astype(F32)
    k = k_ref[0].astype(BF16).astype(F32)
    v = v_ref[0].astype(BF16).astype(F32)
    s = jnp.sum(q * k, axis=-1, keepdims=True)
    p = jnp.exp(s - jnp.max(s, axis=1, keepdims=True))
    p = p / jnp.sum(p, axis=1, keepdims=True)
    p = p.astype(BF16).astype(F32)
    o_ref[...] = jnp.sum(p * v, axis=1, keepdims=True).astype(o_ref.dtype)


def _xattn_sample(layer, qx_s, mem_k, mem_v, nb):
    n, _, heads, dh = qx_s.shape
    n_mem = mem_k.shape[2]
    kv_spec = pl.BlockSpec((1, nb, n_mem, heads, dh), lambda i: (layer, i, 0, 0, 0))
    q_spec = pl.BlockSpec((nb, 1, heads, dh), lambda i: (i, 0, 0, 0))
    return pl.pallas_call(
        _xattn_sample_kernel,
        grid=(n // nb,),
        in_specs=[q_spec, kv_spec, kv_spec],
        out_specs=q_spec,
        out_shape=jax.ShapeDtypeStruct((n, 1, heads, dh), BF16),
        compiler_params=_params("parallel"),
        name="xattn_sample",
    )(qx_s, mem_k, mem_v)


def _route_kernel(x_ref, o_ref, wo_ref, g_ref, wr_hi_ref, wr_lo_ref, br_ref,
                  xo_ref, xn_ref, route_ref):
    x = x_ref[...] + _dot(o_ref[...], wo_ref[...])
    xo_ref[...] = x
    xn = _rms(x, g_ref[...])
    xn_ref[...] = xn
    hi = xn.astype(BF16)
    lo = (xn - hi.astype(F32)).astype(BF16)
    logits = (_dot(hi, wr_hi_ref[...]) + _dot(lo, wr_hi_ref[...]) + _dot(hi, wr_lo_ref[...])) + br_ref[...]

    lane = lax.broadcasted_iota(jnp.int32, logits.shape, 1).astype(F32)
    neg = -jnp.inf
    big = float(ROUTE_LANES)

    def top1(vals):
        vmax = jnp.max(vals, axis=-1, keepdims=True)
        idx = jnp.min(jnp.where(vals == vmax, lane, big), axis=-1, keepdims=True)
        return vmax, idx

    g_vals = jnp.where(lane < N_GROUPS, logits, neg)
    g_max, g_sel = top1(g_vals)
    g_prob = 1.0 / jnp.sum(jnp.exp(g_vals - g_max), axis=-1, keepdims=True)
    e_lo = N_GROUPS + g_sel * EXPERTS_PER_GROUP
    e_vals = jnp.where((lane >= e_lo) & (lane < e_lo + EXPERTS_PER_GROUP), logits, neg)
    v1, i1 = top1(e_vals)
    v2, i2 = top1(jnp.where(lane == i1, neg, e_vals))
    r = jnp.exp(v2 - v1)
    w1 = g_prob / (1.0 + r)
    w2 = g_prob * r / (1.0 + r)
    rec = jnp.where(lane == 0, i1 - N_GROUPS, 0.0)
    rec = jnp.where(lane == 1, i2 - N_GROUPS, rec)
    rec = jnp.where(lane == 2, w1, rec)
    rec = jnp.where(lane == 3, w2, rec)
    route_ref[...] = rec


def _route(x, o, w_o, g, wr_hi, wr_lo, b_r, tm):
    t, d = x.shape
    row = lambda w: pl.BlockSpec((tm, w), lambda i: (i, 0))
    return pl.pallas_call(
        _route_kernel,
        grid=(t // tm,),
        in_specs=[row(d), row(o.shape[1]), _const_spec(w_o.shape), _const_spec((1, d)),
                  _const_spec(wr_hi.shape), _const_spec(wr_lo.shape), _const_spec((1, ROUTE_LANES))],
        out_specs=[row(d), row(d), row(ROUTE_LANES)],
        out_shape=[jax.ShapeDtypeStruct((t, d), F32), jax.ShapeDtypeStruct((t, d), F32),
                   jax.ShapeDtypeStruct((t, ROUTE_LANES), F32)],
        compiler_params=_params("parallel"),
        name="route",
    )(x, o, w_o, g, wr_hi, wr_lo, b_r)


_MOE_ROWS = 512


def _moe_kernel(r_max, tok_ref, dst_ref, tile_e_ref, nt_ref, gate_ref, x_hbm, wg_ref, wu_ref, wd_ref,
                out_hbm, xbuf, xb_ref, ybuf, gsem, ssem):
    del tile_e_ref
    tm = xb_ref.shape[0]
    i = pl.program_id(0)
    nt = nt_ref[0]
    slot = i % 2

    def gather_row(tile, into, r):
        tok = tok_ref[tile * tm + r]
        pltpu.make_async_copy(x_hbm.at[pl.ds(tok, 1)], xbuf.at[into, pl.ds(r, 1)], gsem.at[into]).start()

    def wait_gather(into):
        pltpu.make_async_copy(x_hbm.at[pl.ds(0, tm)], xbuf.at[into], gsem.at[into]).wait()

    def scatter_row(dst, frm, r):
        pltpu.make_async_copy(ybuf.at[frm, pl.ds(r, 1)], out_hbm.at[pl.ds(dst, 1)], ssem.at[frm]).start()

    def wait_scatter(frm):
        pltpu.make_async_copy(ybuf.at[frm], out_hbm.at[pl.ds(0, tm)], ssem.at[frm]).wait()

    def rows_loop(fn):
        def body(r, carry):
            fn(r)
            return carry
        lax.fori_loop(0, tm, body, 0)

    @pl.when(i == 0)
    def _():
        ybuf[...] = jnp.zeros(ybuf.shape, F32)
        rows_loop(lambda r: gather_row(0, 0, r))
        rows_loop(lambda r: scatter_row(r_max + tm + r, 0, r))

    @pl.when(i < nt)
    def _():
        wait_gather(slot)
        xb_ref[...] = xbuf[slot].astype(BF16)
        nxt = jnp.minimum(i + 1, nt - 1)
        for r in range(tm):
            gather_row(nxt, 1 - slot, r)
        for r in range(tm):
            scatter_row(dst_ref[i * tm + r], 1 - slot, r)
        x = xb_ref[...]
        h = _silu(_dot(x, wg_ref[0])) * _dot(x, wu_ref[0])
        y = _dot((h * gate_ref[...]).astype(BF16), wd_ref[0])
        wait_scatter(slot)
        ybuf[slot] = y

    @pl.when(i == nt - 1)
    def _():
        rows_loop(lambda r: scatter_row(dst_ref[(i + 1) * tm + r], slot, r))
        wait_gather(1 - slot)
        wait_scatter(1 - slot)
        wait_scatter(slot)

    @pl.when(i >= nt)
    def _():
        @pl.when(i == nt)
        def _():
            ybuf[0] = jnp.zeros(ybuf.shape[1:], F32)

        fill = pltpu.make_async_copy(ybuf.at[0], out_hbm.at[pl.ds(pl.multiple_of(i * tm, tm), tm)], ssem.at[0])
        fill.start()
        fill.wait()


def _moe(xn, tok, dst, tile_e, n_tiles, gate_rows, w_gate, w_up, w_down):
    t, d = xn.shape
    tm = _MOE_ROWS
    r_max = tok.shape[0]
    f = w_gate.shape[2]
    dst = jnp.concatenate([r_max + jnp.arange(tm, dtype=jnp.int32), dst])

    def w_index(i, tok_r, dst_r, tile_e_r, nt_r):
        return (tile_e_r[jnp.minimum(i, nt_r[0] - 1)], 0, 0)

    grid_spec = pltpu.PrefetchScalarGridSpec(
        num_scalar_prefetch=4,
        grid=(r_max // tm,),
        in_specs=[pl.BlockSpec((tm, 1), lambda i, *_: (i, 0)),
                  pl.BlockSpec(memory_space=pl.ANY),
                  pl.BlockSpec((1, d, f), w_index), pl.BlockSpec((1, d, f), w_index),
                  pl.BlockSpec((1, f, d), w_index)],
        out_specs=pl.BlockSpec(memory_space=pl.ANY),
        scratch_shapes=[pltpu.VMEM((2, tm, d), F32), pltpu.VMEM((tm, d), BF16), pltpu.VMEM((2, tm, d), F32),
                        pltpu.SemaphoreType.DMA((2,)), pltpu.SemaphoreType.DMA((2,))])
    return pl.pallas_call(
        functools.partial(_moe_kernel, r_max),
        grid_spec=grid_spec,
        out_shape=jax.ShapeDtypeStruct((r_max + 2 * tm, d), F32),
        compiler_params=_params("arbitrary", row_dma=True),
        name="moe",
    )(tok, dst, tile_e, n_tiles, gate_rows, xn, w_gate, w_up, w_down)


def _moe_rows_max(t):
    tm = _MOE_ROWS
    return ((2 * t + N_EXPERTS * (tm - 1)) // tm) * tm


def _moe_plan(route, t):
    tm = _MOE_ROWS
    r_max = _moe_rows_max(t)
    e_flat = route[:, :2].astype(jnp.int32).reshape(-1)
    w_flat = route[:, 2:4].reshape(-1)
    onehot = (e_flat[:, None] == jnp.arange(N_EXPERTS)[None, :]).astype(F32)
    ranks = _count_prefix(onehot)
    counts = ranks[-1].astype(jnp.int32)
    rank = jnp.sum(ranks * onehot, axis=1).astype(jnp.int32) - 1
    padded = (counts + tm - 1) // tm * tm
    ends = jnp.cumsum(padded)
    pos = jnp.sum(onehot.astype(jnp.int32) * (ends - padded)[None, :], axis=1) + rank
    n_rows = ends[-1]
    pair = jnp.arange(2 * t, dtype=jnp.int32)
    row_pair = jnp.full((r_max,), -1, jnp.int32).at[pos].set(pair)
    valid = row_pair >= 0
    tok = jnp.where(valid, row_pair // 2, 0)
    n_pad_before = _count_prefix((~valid).astype(F32)[:, None])[:, 0].astype(jnp.int32) - 1
    dst = jnp.where(valid, (row_pair % 2) * t + row_pair // 2, 2 * t + n_pad_before)
    gate_rows = jnp.where(valid, w_flat[jnp.maximum(row_pair, 0)], 0.0)[:, None]
    tile_start = jnp.arange(r_max // tm, dtype=jnp.int32) * tm
    tile_e = jnp.minimum(jnp.sum((ends[None, :] <= tile_start[:, None]).astype(jnp.int32), axis=1), N_EXPERTS - 1)
    n_tiles = (n_rows // tm).astype(jnp.int32).reshape(1)
    return tok.astype(jnp.int32), dst.astype(jnp.int32), tile_e.astype(jnp.int32), n_tiles, gate_rows


def _count_prefix(flags):
    n, k = flags.shape
    blocks = flags.reshape(n // LANES, LANES, k)
    tril = jnp.tril(jnp.ones((LANES, LANES), F32))
    within = jnp.einsum("ij,bjk->bik", tril, blocks)
    totals = within[:, -1, :]
    tril_b = jnp.tril(jnp.ones((n // LANES, n // LANES), F32), -1)
    before = jnp.einsum("ab,bk->ak", tril_b, totals)
    return (within + before[:, None, :]).reshape(n, k)


def _final_kernel(x_ref, y0_ref, y1_ref, g_ref, o_ref):
    o_ref[...] = _rms(x_ref[...] + y0_ref[...] + y1_ref[...], g_ref[...])


def _final(x, ypair, g, tm):
    t, d = x.shape
    nblk = t // tm
    row = pl.BlockSpec((tm, d), lambda i: (i, 0))
    return pl.pallas_call(
        _final_kernel,
        grid=(nblk,),
        in_specs=[row, row, pl.BlockSpec((tm, d), lambda i: (i + nblk, 0)), _const_spec((1, d))],
        out_specs=row,
        out_shape=jax.ShapeDtypeStruct((t, d), F32),
        compiler_params=_params("parallel"),
        name="final_norm",
    )(x, ypair, ypair, g)


def _rope_tables(pos):
    half = MLA_ROPE // 2
    inv = ROPE_THETA ** (-jnp.arange(half, dtype=F32) / half)
    ang = pos.astype(F32)[:, None] * inv[None, :]
    cos, sin = jnp.cos(ang), jnp.sin(ang)
    pad = jnp.zeros((pos.shape[0], LANES - MLA_ROPE), F32)
    return (jnp.concatenate([cos, cos, pad], axis=1), jnp.concatenate([-sin, sin, pad], axis=1))


def _swap_halves(w):
    half = w.shape[-1] // 2
    return jnp.concatenate([w[..., half:], w[..., :half]], axis=-1)


def _pad_lanes(w, width=LANES):
    return jnp.pad(w, [(0, 0)] * (w.ndim - 1) + [(0, width - w.shape[-1])])


def _token_tile(t):
    for tm in (320, 256, 128):
        if t % tm == 0:
            return tm
    raise ValueError(f"token count {t} is not a multiple of {LANES}")


def kernel(x_prompt, x_sample, cache_mla_ckv, cache_mla_krope, cache_fox_k, cache_fox_v, cache_fox_logf, cache_mem_k, cache_mem_v, state_conv, page_table, mem_prompt, norm_mix, w_in, conv_w, conv_b, conv_ln_g, conv_ln_b, mla_q_norm, mla_w_uq, mla_kv_norm, mla_w_uk, mla_w_uv, fox_f_bias, w_out, norm_xattn, norm_mem, xa_w_q, xa_w_kv, xa_w_o, norm_ffn, router_group_w, router_group_b, router_expert_w, router_expert_b, exp_w_gate, exp_w_up, exp_w_down, norm_final):
    batch, seq, d = x_prompt.shape
    n_dec, t_dec, _ = x_sample.shape
    assert t_dec == 1, "the sample group decodes one row per sequence"
    depth = w_in.shape[0]
    n_pages, page = page_table.shape[1], cache_mla_ckv.shape[2]
    assert n_pages % _PAGES_PER_STEP == 0
    tp = batch * seq
    t = tp + n_dec
    tm = _token_tile(t)
    t_att = min(512, seq)
    n_mem = mem_prompt.shape[1]
    e_xa = XA_HEADS * XA_HEAD_DIM

    pos = jnp.concatenate([jnp.tile(jnp.arange(seq), batch), jnp.full((n_dec,), n_pages * page)])
    cos_t, sin_t = _rope_tables(pos)

    sizes = (C_CONV, C_CONV, MLA_Q_RANK, MLA_KV_RANK, MLA_ROPE, FOX_HEADS * FOX_HEAD_DIM,
             FOX_HEAD_DIM, FOX_HEAD_DIM, FOX_HEADS)
    bounds = [0]
    for sz in sizes:
        bounds.append(bounds[-1] + sz)
    wa, wg_, wcq, wckv, wkr, wfq, wfk, wfv, wff = [w_in[:, :, bounds[k]:bounds[k + 1]] for k in range(9)]
    w_in_ext = jnp.concatenate([wa, wg_, wcq, wckv, wfq, wfk, wfv, _pad_lanes(wkr),
                                _pad_lanes(_swap_halves(wkr)), _pad_lanes(wff)], axis=-1).astype(BF16)
    uq_nope = mla_w_uq[..., :MLA_NOPE].reshape(depth, MLA_Q_RANK, MLA_HEADS * MLA_NOPE)
    uq_rope = mla_w_uq[..., MLA_NOPE:]
    w_uq_ext = jnp.concatenate(
        [uq_nope, _pad_lanes(uq_rope).reshape(depth, MLA_Q_RANK, MLA_HEADS * LANES),
         _pad_lanes(_swap_halves(uq_rope)).reshape(depth, MLA_Q_RANK, MLA_HEADS * LANES)], axis=-1).astype(BF16)
    w_ukt = jnp.transpose(mla_w_uk, (0, 2, 3, 1)).astype(BF16)
    w_uv = jnp.transpose(mla_w_uv, (0, 2, 1, 3)).astype(BF16)
    f_bias_row = _pad_lanes(fox_f_bias)[:, None, :]
    w_out_b = w_out.astype(BF16)
    w_q_b = xa_w_q.astype(BF16)
    w_kv_b = xa_w_kv.astype(BF16)
    w_o_b = xa_w_o.astype(BF16)
    w_r = _pad_lanes(jnp.concatenate([router_group_w, router_expert_w], axis=-1), ROUTE_LANES)
    w_r_hi = w_r.astype(BF16)
    w_r_lo = (w_r - w_r_hi.astype(F32)).astype(BF16)
    b_r = _pad_lanes(jnp.concatenate([router_group_b, router_expert_b], axis=-1), ROUTE_LANES)[:, None, :]
    w_gate_b = exp_w_gate.astype(BF16)
    w_up_b = exp_w_up.astype(BF16)
    w_down_b = exp_w_down.astype(BF16)
    row = lambda v: v[:, None, :]
    norm_mix_r, norm_xattn_r, norm_ffn_r, norm_mem_r = row(norm_mix), row(norm_xattn), row(norm_ffn), row(norm_mem)
    q_norm_r, kv_norm_r = row(mla_q_norm), row(mla_kv_norm)
    conv_b_r, ln_g_r, ln_b_r = row(conv_b), row(conv_ln_g), row(conv_ln_b)

    mk, mv, mk_b, mv_b = _memkv(mem_prompt.reshape(batch * n_mem, d), norm_mem_r, w_kv_b, n_mem)
    mk_b = mk_b.reshape(depth, batch, n_mem, e_xa)
    mv_b = mv_b.reshape(depth, batch, n_mem, e_xa)

    x = jnp.concatenate([x_prompt.reshape(tp, d), x_sample.reshape(n_dec, d)], axis=0)
    ypair = None
    leaves = {k: [] for k in ("ckv", "kr", "fk", "fv", "logf", "conv_p", "conv_s")}
    for l in range(depth):
        (x, u, ckv, kr, fk, fv, logf, kcat, qcat, fq_b, fk_b, fv_b) = _proj(
            x, ypair, norm_mix_r[l], w_in_ext[l], w_uq_ext[l], w_ukt[l], q_norm_r[l], kv_norm_r[l],
            f_bias_row[l], cos_t, sin_t, tm)
        for name, val in (("ckv", ckv), ("kr", kr), ("fk", fk), ("fv", fv), ("logf", logf)):
            leaves[name].append(val)
        u_p = u[:tp].reshape(batch, seq, C_CONV)
        u_s = u[tp:]
        leaves["conv_p"].append(u_p[:, seq - (CONV_WIDTH - 1):])
        leaves["conv_s"].append(jnp.concatenate([state_conv[l][:, 1:], u_s[:, None, :]], axis=1))
        conv_p = _conv_prompt(u, batch, seq, conv_w[l], conv_b_r[l], ln_g_r[l], ln_b_r[l], min(256, seq))
        conv_s = _conv_sample(state_conv[l], u_s, conv_w[l], conv_b_r[l], ln_g_r[l], ln_b_r[l], min(32, n_dec))
        lat_p = _flash(qcat, kcat, kcat, None, None, batch, seq, MLA_HEADS, MLA_QK, MLA_KV_RANK, t_att)
        lf_t = jnp.pad(jnp.transpose(logf[:tp].reshape(batch, seq, FOX_HEADS), (0, 2, 1)),
                       ((0, 0), (0, HEADS_PAD - FOX_HEADS), (0, 0)))
        cum_row = _cumsum_time(lf_t)
        cum_col = jnp.transpose(cum_row, (0, 2, 1)).reshape(tp, HEADS_PAD)
        fox_p = _flash(fq_b, fk_b, fv_b, cum_col, cum_row, batch, seq, FOX_HEADS, FOX_HEAD_DIM, FOX_HEAD_DIM, t_att)
        pad_heads = lambda a: jnp.pad(a, ((0, 0), (0, HEADS_PAD - a.shape[1]), (0, 0)))
        q_s = pad_heads(qcat[tp:].reshape(n_dec, MLA_HEADS, MLA_QK))
        fq_s = pad_heads(fq_b[tp:].reshape(n_dec, FOX_HEADS, FOX_HEAD_DIM))
        cn = pad_heads(logf[tp:][:, :, None])
        lat_s, fox_s = _decode(l, page_table, q_s, fq_s, kcat[tp:][:, None, :], fk_b[tp:][:, None, :],
                               fv_b[tp:][:, None, :], cn, cache_mla_ckv, cache_mla_krope, cache_fox_k,
                               cache_fox_v, cache_fox_logf)
        lat_s = lat_s[:, :MLA_HEADS].reshape(n_dec, MLA_HEADS * MLA_KV_RANK)
        fox_s = fox_s[:, :FOX_HEADS].reshape(n_dec, FOX_HEADS * FOX_HEAD_DIM)
        x, qx = _merge(x, jnp.concatenate([conv_p, conv_s]), jnp.concatenate([lat_p, lat_s]),
                       jnp.concatenate([fox_p, fox_s]), w_uv[l], w_out_b[l], norm_xattn_r[l], w_q_b[l], tm)
        o_p = _xattn_prompt(l, qx, mk_b, mv_b, batch, seq, min(512, seq))
        o_s = _xattn_sample(l, qx[tp:].reshape(n_dec, 1, XA_HEADS, XA_HEAD_DIM), cache_mem_k, cache_mem_v,
                            min(4, n_dec))
        x, xn, route = _route(x, jnp.concatenate([o_p, o_s.reshape(n_dec, e_xa)]), w_o_b[l], norm_ffn_r[l],
                              w_r_hi[l], w_r_lo[l], b_r[l], tm)
        tok, dst, tile_e, n_tiles, gate_rows = _moe_plan(route, t)
        ypair = _moe(xn, tok, dst, tile_e, n_tiles, gate_rows, w_gate_b[l], w_up_b[l], w_down_b[l])
    y = _final(x, ypair, norm_final[None, :], tm)

    def stack_p(name, width):
        return jnp.stack([v[:tp].reshape(batch, seq, width) for v in leaves[name]])

    def stack_s(name, width):
        return jnp.stack([v[tp:].reshape(n_dec, 1, width) for v in leaves[name]])

    mem_shape = (depth, batch, n_mem, XA_HEADS, XA_HEAD_DIM)
    return (y[:tp].reshape(batch, seq, d), y[tp:].reshape(n_dec, 1, d),
            stack_p("ckv", MLA_KV_RANK), stack_p("kr", MLA_ROPE), stack_p("fk", FOX_HEAD_DIM),
            stack_p("fv", FOX_HEAD_DIM), stack_p("logf", FOX_HEADS),
            mk.reshape(mem_shape), mv.reshape(mem_shape), jnp.stack(leaves["conv_p"]),
            stack_s("ckv", MLA_KV_RANK), stack_s("kr", MLA_ROPE), stack_s("fk", FOX_HEAD_DIM),
            stack_s("fv", FOX_HEAD_DIM), stack_s("logf", FOX_HEADS), jnp.stack(leaves["conv_s"]))
```

```python
import functools

import jax
import jax.numpy as jnp
from jax import lax
from jax.experimental import pallas as pl
from jax.experimental.pallas import tpu as pltpu

F32 = jnp.float32
BF16 = jnp.bfloat16

EPS = 1e-6
ROPE_THETA = 10000.0
C_CONV = 512
CONV_WIDTH = 31
MLA_HEADS = 6
MLA_NOPE = 128
MLA_ROPE = 64
MLA_Q_RANK = 512
MLA_KV_RANK = 256
FOX_HEADS = 6
FOX_HEAD_DIM = 128
XA_HEADS = 4
XA_HEAD_DIM = 128
N_GROUPS = 4
EXPERTS_PER_GROUP = 4
N_EXPERTS = 16
D_EXPERT = 512
MLA_SCALE = (MLA_NOPE + MLA_ROPE) ** -0.5
FOX_SCALE = FOX_HEAD_DIM ** -0.5
XA_SCALE = XA_HEAD_DIM ** -0.5

LANES = 128
SUBLANES = 8
VMEM_LIMIT_BYTES = 56 * 1024 * 1024

HEADS_PAD = SUBLANES
MLA_QK = 3 * LANES

_SEG_A = 0
_SEG_G = _SEG_A + C_CONV
_SEG_CQ = _SEG_G + C_CONV
_SEG_CKV = _SEG_CQ + MLA_Q_RANK
_SEG_FQ = _SEG_CKV + MLA_KV_RANK
_SEG_FK = _SEG_FQ + FOX_HEADS * FOX_HEAD_DIM
_SEG_FV = _SEG_FK + FOX_HEAD_DIM
_SEG_KR = _SEG_FV + FOX_HEAD_DIM
_SEG_KRS = _SEG_KR + LANES
_SEG_FF = _SEG_KRS + LANES
D_IN_EXT = _SEG_FF + LANES

_UQ_NOPE = 0
_UQ_ROPE = MLA_HEADS * MLA_NOPE
_UQ_ROPES = _UQ_ROPE + MLA_HEADS * LANES
D_UQ_EXT = _UQ_ROPES + MLA_HEADS * LANES

ROUTE_LANES = LANES


def _params(*sem, row_dma=False):
    return pltpu.CompilerParams(dimension_semantics=sem, vmem_limit_bytes=VMEM_LIMIT_BYTES,
                                disable_bounds_checks=row_dma)


def _const_spec(shape):
    nd = len(shape)
    return pl.BlockSpec(shape, lambda *_: (0,) * nd, pipeline_mode=pl.Buffered(1))


def _rms(x, g):
    return x * lax.rsqrt(jnp.mean(x * x, axis=-1, keepdims=True) + EPS) * g


def _dot(a, b):
    return jnp.dot(a, b, preferred_element_type=F32)


def _dot_nt(a, b):
    return lax.dot_general(a, b, (((1,), (1,)), ((), ())), preferred_element_type=F32)


def _sigmoid(x):
    return 1.0 / (1.0 + jnp.exp(-x))


def _silu(x):
    return x * _sigmoid(x)


def _log_sigmoid(z):
    return jnp.minimum(z, 0.0) - jnp.log(1.0 + jnp.exp(-jnp.abs(z)))


def _split3(x):
    hi = x.astype(BF16)
    r1 = x - hi.astype(F32)
    mid = r1.astype(BF16)
    lo = (r1 - mid.astype(F32)).astype(BF16)
    return hi, mid, lo


def _proj_kernel(has_y, *refs):
    if has_y:
        x_ref, y0_ref, y1_ref = refs[:3]
        refs = refs[3:]
    else:
        x_ref = refs[0]
        refs = refs[1:]
    (g_ref, win_ref, wuq_ref, wukt_ref, qn_ref, kvn_ref, fb_ref, cos_ref, sin_ref,
     xo_ref, u_ref, ckv_ref, kr_ref, fk_ref, fv_ref, logf_ref,
     kcat_ref, qcat_ref, fqb_ref, fkb_ref, fvb_ref) = refs

    x = x_ref[...]
    if has_y:
        x = x + y0_ref[...] + y1_ref[...]
    xo_ref[...] = x
    xb = _rms(x, g_ref[...]).astype(BF16)

    def seg(lo, width):
        return _dot(xb, win_ref[:, lo:lo + width])

    u_ref[...] = seg(_SEG_A, C_CONV) * _sigmoid(seg(_SEG_G, C_CONV))
    ckv = _rms(seg(_SEG_CKV, MLA_KV_RANK), kvn_ref[...])
    ckv_ref[...] = ckv
    cos = cos_ref[...]
    sin = sin_ref[...]
    kr = seg(_SEG_KR, LANES) * cos + seg(_SEG_KRS, LANES) * sin
    kr_ref[...] = kr[:, :MLA_ROPE]
    kcat_ref[:, :MLA_KV_RANK] = ckv.astype(BF16)
    kcat_ref[:, MLA_KV_RANK:] = kr.astype(BF16)
    cqn = _rms(seg(_SEG_CQ, MLA_Q_RANK), qn_ref[...]).astype(BF16)
    q = _dot(cqn, wuq_ref[...])
    for h in range(MLA_HEADS):
        qn = q[:, _UQ_NOPE + h * MLA_NOPE:_UQ_NOPE + (h + 1) * MLA_NOPE].astype(BF16)
        ql = _dot(qn, wukt_ref[h]) * MLA_SCALE
        qr = (q[:, _UQ_ROPE + h * LANES:_UQ_ROPE + (h + 1) * LANES] * cos
              + q[:, _UQ_ROPES + h * LANES:_UQ_ROPES + (h + 1) * LANES] * sin) * MLA_SCALE
        qcat_ref[:, h * MLA_QK:h * MLA_QK + MLA_KV_RANK] = ql.astype(BF16)
        qcat_ref[:, h * MLA_QK + MLA_KV_RANK:(h + 1) * MLA_QK] = qr.astype(BF16)
    fqb_ref[...] = (seg(_SEG_FQ, FOX_HEADS * FOX_HEAD_DIM) * FOX_SCALE).astype(BF16)
    fk = seg(_SEG_FK, FOX_HEAD_DIM)
    fv = seg(_SEG_FV, FOX_HEAD_DIM)
    fk_ref[...] = fk
    fv_ref[...] = fv
    fkb_ref[...] = fk.astype(BF16)
    fvb_ref[...] = fv.astype(BF16)
    logf = _log_sigmoid(seg(_SEG_FF, LANES) + fb_ref[...])
    logf_ref[...] = logf[:, :FOX_HEADS]


def _proj(x, ypair, g, w_in_ext, w_uq_ext, w_ukt, q_norm, kv_norm, f_bias_row, cos_t, sin_t, tm):
    t, d = x.shape
    nblk = t // tm
    has_y = ypair is not None
    row = lambda w: pl.BlockSpec((tm, w), lambda i: (i, 0))
    in_specs = [row(d)]
    args = [x]
    if has_y:
        in_specs += [row(d), pl.BlockSpec((tm, d), lambda i: (i + nblk, 0))]
        args += [ypair, ypair]
    in_specs += [_const_spec((1, d)), _const_spec(w_in_ext.shape), _const_spec(w_uq_ext.shape),
                 _const_spec(w_ukt.shape), _const_spec((1, MLA_Q_RANK)), _const_spec((1, MLA_KV_RANK)),
                 _const_spec((1, LANES)), row(LANES), row(LANES)]
    args += [g, w_in_ext, w_uq_ext, w_ukt, q_norm, kv_norm, f_bias_row, cos_t, sin_t]
    widths = [(d, F32), (C_CONV, F32), (MLA_KV_RANK, F32), (MLA_ROPE, F32), (FOX_HEAD_DIM, F32),
              (FOX_HEAD_DIM, F32), (FOX_HEADS, F32), (MLA_QK, BF16), (MLA_HEADS * MLA_QK, BF16),
              (FOX_HEADS * FOX_HEAD_DIM, BF16), (FOX_HEAD_DIM, BF16), (FOX_HEAD_DIM, BF16)]
    return pl.pallas_call(
        functools.partial(_proj_kernel, has_y),
        grid=(nblk,),
        in_specs=in_specs,
        out_specs=[row(w) for w, _ in widths],
        out_shape=[jax.ShapeDtypeStruct((t, w), dt) for w, dt in widths],
        compiler_params=_params("parallel"),
        name="proj",
    )(*args)


def _cumsum_kernel(lf_ref, tri_ref, o_ref):
    n_chunks = lf_ref.shape[2] // LANES
    tri = tri_ref[...]
    carry = jnp.zeros((HEADS_PAD, 1), F32)
    for c in range(n_chunks):
        hi, mid, lo = _split3(lf_ref[0, :, c * LANES:(c + 1) * LANES])
        local = _dot(hi, tri) + _dot(mid, tri) + _dot(lo, tri)
        cum = local + carry
        o_ref[0, :, c * LANES:(c + 1) * LANES] = cum
        carry = cum[:, LANES - 1:LANES]


def _cumsum_time(lf_t):
    b, hp, s = lf_t.shape
    tri = (jnp.arange(LANES)[:, None] <= jnp.arange(LANES)[None, :]).astype(BF16)
    return pl.pallas_call(
        _cumsum_kernel,
        grid=(b,),
        in_specs=[pl.BlockSpec((1, hp, s), lambda i: (i, 0, 0)), _const_spec((LANES, LANES))],
        out_specs=pl.BlockSpec((1, hp, s), lambda i: (i, 0, 0)),
        out_shape=jax.ShapeDtypeStruct((b, hp, s), F32),
        compiler_params=_params("parallel"),
        name="cumsum",
    )(lf_t, tri)


_CONV_HALO = 32
_CONV_ROWS = 64


def _layer_norm_silu(y, g, b):
    mu = jnp.mean(y, axis=-1, keepdims=True)
    yc = y - mu
    var = jnp.mean(yc * yc, axis=-1, keepdims=True)
    return _silu(yc * lax.rsqrt(var + EPS) * g + b)


def _conv_prompt_kernel(blocks_per_seq, u_ref, halo_ref, w_ref, b_ref, g_ref, be_ref, o_ref, ext_ref):
    tm = u_ref.shape[0]
    first = (pl.program_id(0) % blocks_per_seq) == 0
    ext_ref[:_CONV_HALO, :] = jnp.where(first, 0.0, halo_ref[...])
    ext_ref[_CONV_HALO:, :] = u_ref[...]
    off = _CONV_HALO - (CONV_WIDTH - 1)
    for r0 in range(0, tm, _CONV_ROWS):
        acc = jnp.zeros((_CONV_ROWS, C_CONV), F32) + b_ref[...]
        for k in range(CONV_WIDTH):
            acc = acc + ext_ref[r0 + k + off:r0 + k + off + _CONV_ROWS, :] * w_ref[k:k + 1, :]
        o_ref[r0:r0 + _CONV_ROWS, :] = _layer_norm_silu(acc, g_ref[...], be_ref[...]).astype(o_ref.dtype)


def _conv_prompt(u, batch, seq, conv_w, conv_b, ln_g, ln_b, tm):
    n_rows = batch * seq
    nblk = n_rows // tm
    per_halo = tm // _CONV_HALO
    return pl.pallas_call(
        functools.partial(_conv_prompt_kernel, seq // tm),
        grid=(nblk,),
        in_specs=[pl.BlockSpec((tm, C_CONV), lambda i: (i, 0)),
                  pl.BlockSpec((_CONV_HALO, C_CONV), lambda i: (jnp.maximum(i * per_halo - 1, 0), 0)),
                  _const_spec((CONV_WIDTH, C_CONV)), _const_spec((1, C_CONV)),
                  _const_spec((1, C_CONV)), _const_spec((1, C_CONV))],
        out_specs=pl.BlockSpec((tm, C_CONV), lambda i: (i, 0)),
        out_shape=jax.ShapeDtypeStruct((n_rows, C_CONV), BF16),
        scratch_shapes=[pltpu.VMEM((tm + _CONV_HALO, C_CONV), F32)],
        compiler_params=_params("parallel"),
        name="conv_prompt",
    )(u, u, conv_w, conv_b, ln_g, ln_b)


def _conv_sample_kernel(st_ref, u_ref, w_ref, b_ref, g_ref, be_ref, o_ref):
    w = w_ref[...]
    hist = jnp.sum(st_ref[...] * w[None, :CONV_WIDTH - 1, :], axis=1)
    y = hist + u_ref[...] * w[CONV_WIDTH - 1:CONV_WIDTH, :] + b_ref[...]
    o_ref[...] = _layer_norm_silu(y, g_ref[...], be_ref[...]).astype(o_ref.dtype)


def _conv_sample(state, u_s, conv_w, conv_b, ln_g, ln_b, nb):
    n = state.shape[0]
    return pl.pallas_call(
        _conv_sample_kernel,
        grid=(n // nb,),
        in_specs=[pl.BlockSpec((nb, CONV_WIDTH - 1, C_CONV), lambda i: (i, 0, 0)),
                  pl.BlockSpec((nb, C_CONV), lambda i: (i, 0)),
                  _const_spec((CONV_WIDTH, C_CONV)), _const_spec((1, C_CONV)),
                  _const_spec((1, C_CONV)), _const_spec((1, C_CONV))],
        out_specs=pl.BlockSpec((nb, C_CONV), lambda i: (i, 0)),
        out_shape=jax.ShapeDtypeStruct((n, C_CONV), BF16),
        compiler_params=_params("parallel"),
        name="conv_sample",
    )(state, u_s, conv_w, conv_b, ln_g, ln_b)


def _flash_kernel(n_heads, dq, dv, has_bias, *refs):
    if has_bias:
        q_ref, k_ref, v_ref, ci_ref, cs_ref, o_ref, m_ref, l_ref, acc_ref, ci_rep_ref = refs
    else:
        q_ref, k_ref, v_ref, o_ref, m_ref, l_ref, acc_ref = refs
    i = pl.program_id(1)
    j = pl.program_id(2)
    tq = q_ref.shape[0]
    tk = k_ref.shape[0]

    def lanes(x, width):
        return jnp.concatenate([x] * (width // LANES), axis=1)

    @pl.when(j == 0)
    def _():
        m_ref[...] = jnp.full(m_ref.shape, -jnp.inf, F32)
        l_ref[...] = jnp.zeros(l_ref.shape, F32)
        acc_ref[...] = jnp.zeros(acc_ref.shape, F32)
        if has_bias:
            for h in range(n_heads):
                ci_rep_ref[h] = jnp.broadcast_to(ci_ref[:, h:h + 1], (tq, LANES))

    def update(masked):
        k = k_ref[...]
        v = v_ref[...]
        if masked:
            keep = (lax.broadcasted_iota(jnp.int32, (tq, tk), 1)
                    <= lax.broadcasted_iota(jnp.int32, (tq, tk), 0))
        for h in range(n_heads):
            s = _dot_nt(q_ref[:, h * dq:(h + 1) * dq], k)
            if has_bias:
                s = s + (lanes(ci_rep_ref[h], tk) - cs_ref[0, h:h + 1, :])
            if masked:
                s = jnp.where(keep, s, -jnp.inf)
            m_prev = m_ref[h]
            m_new = jnp.maximum(m_prev, jnp.max(s, axis=-1, keepdims=True))
            alpha = jnp.exp(m_prev - m_new)
            p = jnp.exp(s - lanes(m_new, tk))
            l_ref[h] = alpha * l_ref[h] + jnp.sum(p, axis=-1, keepdims=True)
            acc_ref[h] = lanes(alpha, dv) * acc_ref[h] + _dot(p.astype(BF16), v)
            m_ref[h] = m_new

    @pl.when(j < i)
    def _():
        update(False)

    @pl.when(j == i)
    def _():
        update(True)
        for h in range(n_heads):
            o_ref[:, h * dv:(h + 1) * dv] = (acc_ref[h] / lanes(l_ref[h], dv)).astype(o_ref.dtype)


def _flash(q, k, v, ci, cs, batch, seq, n_heads, dq, dv, tq):
    nq = seq // tq
    has_bias = ci is not None
    in_specs = [pl.BlockSpec((tq, n_heads * dq), lambda b, i, j: (b * nq + i, 0)),
                pl.BlockSpec((tq, dq), lambda b, i, j: (b * nq + jnp.minimum(i, j), 0)),
                pl.BlockSpec((tq, dv), lambda b, i, j: (b * nq + jnp.minimum(i, j), 0))]
    args = [q, k, v]
    if has_bias:
        in_specs += [pl.BlockSpec((tq, HEADS_PAD), lambda b, i, j: (b * nq + i, 0)),
                     pl.BlockSpec((1, HEADS_PAD, tq), lambda b, i, j: (b, 0, jnp.minimum(i, j)))]
        args += [ci, cs]
    return pl.pallas_call(
        functools.partial(_flash_kernel, n_heads, dq, dv, has_bias),
        grid=(batch, nq, nq),
        in_specs=in_specs,
        out_specs=pl.BlockSpec((tq, n_heads * dv), lambda b, i, j: (b * nq + i, 0)),
        out_shape=jax.ShapeDtypeStruct((batch * seq, n_heads * dv), BF16),
        scratch_shapes=[pltpu.VMEM((n_heads, tq, LANES), F32), pltpu.VMEM((n_heads, tq, LANES), F32),
                        pltpu.VMEM((n_heads, tq, dv), F32)]
        + ([pltpu.VMEM((n_heads, tq, LANES), F32)] if has_bias else []),
        compiler_params=_params("parallel", "parallel", "arbitrary"),
        name="flash_bias" if has_bias else "flash",
    )(*args)


_PAGES_PER_STEP = 32


def _decode_kernel(layer, n_pg, n_pages, pt_ref, qlat_t_ref, fq_t_ref, q_ref, fq_ref,
                   knew_ref, fknew_ref, fvnew_ref, cn_ref,
                   ckv_hbm, krt_hbm, fk_hbm, fv_hbm, lft_hbm,
                   ml_ref, fo_ref,
                   ckv_buf, krt_buf, fk_buf, fv_buf, lf_buf, sem,
                   m1_ref, l1_ref, a1_ref, m2_ref, l2_ref, a2_ref, carry_ref):
    step = pl.program_id(0)
    n_steps = pl.num_programs(0)
    n_chunks = n_pages // n_pg
    c = step % n_chunks
    slot = step % 2
    page = ckv_buf.shape[2]
    rows = n_pg * page
    pools = ((ckv_hbm, ckv_buf), (krt_hbm, krt_buf), (fk_hbm, fk_buf), (fv_hbm, fv_buf))

    def lf_copy(pg, into, g):
        return pltpu.make_async_copy(lft_hbm.at[layer, pl.ds(0, FOX_HEADS), pg],
                                     lf_buf.at[into, g, pl.ds(0, FOX_HEADS)], sem.at[into])

    def start_fetch(target, into):
        base = (target // n_chunks) * n_pages + (n_chunks - 1 - target % n_chunks) * n_pg
        for g in range(n_pg):
            pg = pt_ref[base + g]
            for hbm, buf in pools:
                pltpu.make_async_copy(hbm.at[layer, pg], buf.at[into, g], sem.at[into]).start()
            lf_copy(pg, into, g).start()

    def wait_fetch(into):
        for hbm, buf in pools:
            pltpu.make_async_copy(hbm.at[layer, pl.ds(0, n_pg)], buf.at[into], sem.at[into]).wait()
        for g in range(n_pg):
            lf_copy(0, into, g).wait()

    @pl.when(step == 0)
    def _():
        lf_buf[...] = jnp.zeros(lf_buf.shape, F32)
        start_fetch(0, 0)

    @pl.when(c == 0)
    def _():
        m1_ref[...] = jnp.full(m1_ref.shape, -jnp.inf, F32)
        m2_ref[...] = jnp.full(m2_ref.shape, -jnp.inf, F32)
        l1_ref[...] = jnp.zeros(l1_ref.shape, F32)
        l2_ref[...] = jnp.zeros(l2_ref.shape, F32)
        a1_ref[...] = jnp.zeros(a1_ref.shape, F32)
        a2_ref[...] = jnp.zeros(a2_ref.shape, F32)
        carry_ref[...] = jnp.zeros(carry_ref.shape, F32)

    wait_fetch(slot)
    start_fetch(jnp.where(step + 1 < n_steps, step + 1, 0), 1 - slot)

    q = q_ref[0]
    fq = fq_ref[0]
    cn = cn_ref[0]

    def heads_first(st):
        return jnp.transpose(st)[:HEADS_PAD]

    def online(s, v, m_ref, l_ref, a_ref):
        m_prev = m_ref[...]
        m_new = jnp.maximum(m_prev, jnp.max(s, axis=-1, keepdims=True))
        alpha = jnp.exp(m_prev - m_new)
        p = jnp.exp(s - m_new)
        l_ref[...] = alpha * l_ref[...] + jnp.sum(p, axis=-1, keepdims=True)
        a_ref[...] = alpha * a_ref[...] + _dot(p.astype(BF16), v)
        m_ref[...] = m_new

    lf = lf_buf[slot].reshape(n_pg * HEADS_PAD, page)
    lane = lax.broadcasted_iota(jnp.int32, lf.shape, 1)
    incl = lf
    d = 1
    while d < page:
        incl = incl + jnp.where(lane + d < page, pltpu.roll(incl, page - d, 1), 0.0)
        d *= 2
    excl = incl - lf
    carry = carry_ref[...][:, :1]
    bias = [None] * n_pg
    for g in reversed(range(n_pg)):
        bias[g] = excl[g * HEADS_PAD:(g + 1) * HEADS_PAD] + (carry + cn)
        carry = carry + incl[g * HEADS_PAD:(g + 1) * HEADS_PAD, :1]
    carry_ref[...] = jnp.broadcast_to(carry, carry_ref.shape)

    kc = ckv_buf[slot].reshape(rows, MLA_KV_RANK).astype(BF16)
    krt = jnp.concatenate([krt_buf[slot, g].astype(BF16) for g in range(n_pg)], axis=1)
    fkc = fk_buf[slot].reshape(rows, FOX_HEAD_DIM).astype(BF16)
    s1 = heads_first(_dot(kc, qlat_t_ref[0])) + _dot(q[:, MLA_KV_RANK:MLA_KV_RANK + MLA_ROPE], krt)
    s2 = heads_first(_dot(fkc, fq_t_ref[0])) + jnp.concatenate(bias, axis=1)
    online(s1, kc, m1_ref, l1_ref, a1_ref)
    online(s2, fv_buf[slot].reshape(rows, FOX_HEAD_DIM).astype(BF16), m2_ref, l2_ref, a2_ref)

    @pl.when(step == n_steps - 1)
    def _():
        wait_fetch(1 - slot)

    @pl.when(c == n_chunks - 1)
    def _():
        def finish(s_new, v_new, m_ref, l_ref, a_ref, o_ref):
            m_prev = m_ref[...]
            m_new = jnp.maximum(m_prev, s_new)
            alpha = jnp.exp(m_prev - m_new)
            p = jnp.exp(s_new - m_new)
            l = alpha * l_ref[...] + p
            o_ref[0] = ((alpha * a_ref[...] + p * v_new) / l).astype(o_ref.dtype)

        k_new = knew_ref[0].astype(F32)
        s_new = jnp.sum(q.astype(F32) * k_new, axis=-1, keepdims=True)
        finish(s_new, k_new[:, :MLA_KV_RANK], m1_ref, l1_ref, a1_ref, ml_ref)
        s_new = jnp.sum(fq.astype(F32) * fknew_ref[0].astype(F32), axis=-1, keepdims=True)
        finish(s_new, fvnew_ref[0].astype(F32), m2_ref, l2_ref, a2_ref, fo_ref)


def _decode(layer, page_table, q_s, fq_s, k_new, fk_new, fv_new, cn, ckv_pool, kr_pool, fk_pool, fv_pool, lf_pool):
    n, n_pages = page_table.shape
    page = ckv_pool.shape[2]
    n_pg = _PAGES_PER_STEP
    n_chunks = n_pages // n_pg
    pt_flat = page_table.reshape(-1)
    qlat_t = _pad_lanes(jnp.transpose(q_s[:, :, :MLA_KV_RANK], (0, 2, 1)))
    fq_t = _pad_lanes(jnp.transpose(fq_s, (0, 2, 1)))
    krt_pool = jnp.transpose(kr_pool, (0, 1, 3, 2))
    lft_pool = jnp.transpose(lf_pool, (0, 3, 1, 2))

    def per_sample(shape):
        return pl.BlockSpec((1,) + shape, lambda s, pt: (s // n_chunks, 0, 0))

    in_specs = [per_sample((MLA_KV_RANK, LANES)), per_sample((FOX_HEAD_DIM, LANES)),
                per_sample((HEADS_PAD, MLA_QK)), per_sample((HEADS_PAD, FOX_HEAD_DIM)),
                per_sample((1, MLA_QK)), per_sample((1, FOX_HEAD_DIM)), per_sample((1, FOX_HEAD_DIM)),
                per_sample((HEADS_PAD, 1))] + [pl.BlockSpec(memory_space=pl.ANY)] * 5
    buf = lambda width: pltpu.VMEM((2, n_pg, page, width), F32)
    grid_spec = pltpu.PrefetchScalarGridSpec(
        num_scalar_prefetch=1,
        grid=(n * n_chunks,),
        in_specs=in_specs,
        out_specs=[per_sample((HEADS_PAD, MLA_KV_RANK)), per_sample((HEADS_PAD, FOX_HEAD_DIM))],
        scratch_shapes=[buf(MLA_KV_RANK), pltpu.VMEM((2, n_pg, MLA_ROPE, page), F32), buf(FOX_HEAD_DIM),
                        buf(FOX_HEAD_DIM), pltpu.VMEM((2, n_pg, HEADS_PAD, page), F32),
                        pltpu.SemaphoreType.DMA((2,)),
                        pltpu.VMEM((HEADS_PAD, 1), F32), pltpu.VMEM((HEADS_PAD, 1), F32),
                        pltpu.VMEM((HEADS_PAD, MLA_KV_RANK), F32),
                        pltpu.VMEM((HEADS_PAD, 1), F32), pltpu.VMEM((HEADS_PAD, 1), F32),
                        pltpu.VMEM((HEADS_PAD, FOX_HEAD_DIM), F32),
                        pltpu.VMEM((HEADS_PAD, LANES), F32)])
    return pl.pallas_call(
        functools.partial(_decode_kernel, layer, n_pg, n_pages),
        grid_spec=grid_spec,
        out_shape=[jax.ShapeDtypeStruct((n, HEADS_PAD, MLA_KV_RANK), BF16),
                   jax.ShapeDtypeStruct((n, HEADS_PAD, FOX_HEAD_DIM), BF16)],
        compiler_params=_params("arbitrary", row_dma=True),
        name="decode",
    )(pt_flat, qlat_t, fq_t, q_s, fq_s, k_new, fk_new, fv_new, cn,
      ckv_pool, krt_pool, fk_pool, fv_pool, lft_pool)


def _merge_kernel(x_ref, conv_ref, lat_ref, fox_ref, wuv_ref, wout_ref, g_ref, wq_ref,
                  xo_ref, qx_ref, cat_ref):
    cat_ref[:, :C_CONV] = conv_ref[...]
    for h in range(MLA_HEADS):
        o = _dot(lat_ref[:, h * MLA_KV_RANK:(h + 1) * MLA_KV_RANK], wuv_ref[h])
        cat_ref[:, C_CONV + h * MLA_NOPE:C_CONV + (h + 1) * MLA_NOPE] = o.astype(BF16)
    base = C_CONV + MLA_HEADS * MLA_NOPE
    cat_ref[:, base:] = fox_ref[...]
    x = x_ref[...] + _dot(cat_ref[...], wout_ref[...])
    xo_ref[...] = x
    xb = _rms(x, g_ref[...]).astype(BF16)
    qx_ref[...] = (_dot(xb, wq_ref[...]) * XA_SCALE).astype(BF16)


def _merge(x, conv_o, lat, fox_o, w_uv, w_out, g, w_q, tm):
    t, d = x.shape
    row = lambda w: pl.BlockSpec((tm, w), lambda i: (i, 0))
    dq = w_q.shape[1]
    return pl.pallas_call(
        _merge_kernel,
        grid=(t // tm,),
        in_specs=[row(d), row(conv_o.shape[1]), row(lat.shape[1]), row(fox_o.shape[1]),
                  _const_spec(w_uv.shape), _const_spec(w_out.shape), _const_spec((1, d)), _const_spec(w_q.shape)],
        out_specs=[row(d), row(dq)],
        out_shape=[jax.ShapeDtypeStruct((t, d), F32), jax.ShapeDtypeStruct((t, dq), BF16)],
        scratch_shapes=[pltpu.VMEM((tm, w_out.shape[0]), BF16)],
        compiler_params=_params("parallel"),
        name="merge",
    )(x, conv_o, lat, fox_o, w_uv, w_out, g, w_q)


def _memkv_kernel(mem_ref, g_ref, w_ref, k_ref, v_ref, kb_ref, vb_ref):
    kv = _dot(_rms(mem_ref[...], g_ref[0]).astype(BF16), w_ref[0])
    half = kv.shape[1] // 2
    k_ref[0] = kv[:, :half]
    v_ref[0] = kv[:, half:]
    kb_ref[0] = kv[:, :half].astype(BF16)
    vb_ref[0] = kv[:, half:].astype(BF16)


def _memkv(mem, norm_mem, w_kv, tm):
    m, d = mem.shape
    depth, _, two_e = w_kv.shape
    e = two_e // 2
    out = pl.BlockSpec((1, tm, e), lambda l, i: (l, i, 0))
    return pl.pallas_call(
        _memkv_kernel,
        grid=(depth, m // tm),
        in_specs=[pl.BlockSpec((tm, d), lambda l, i: (i, 0)),
                  pl.BlockSpec((1, 1, d), lambda l, i: (l, 0, 0)),
                  pl.BlockSpec((1, d, two_e), lambda l, i: (l, 0, 0))],
        out_specs=[out, out, out, out],
        out_shape=[jax.ShapeDtypeStruct((depth, m, e), F32), jax.ShapeDtypeStruct((depth, m, e), F32),
                   jax.ShapeDtypeStruct((depth, m, e), BF16), jax.ShapeDtypeStruct((depth, m, e), BF16)],
        compiler_params=_params("parallel", "parallel"),
        name="memkv",
    )(mem, norm_mem, w_kv)


def _xattn_prompt_kernel(q_ref, k_ref, v_ref, o_ref):
    for h in range(XA_HEADS):
        sl = slice(h * XA_HEAD_DIM, (h + 1) * XA_HEAD_DIM)
        s = _dot_nt(q_ref[:, sl], k_ref[0, 0, :, sl])
        p = jnp.exp(s - jnp.max(s, axis=-1, keepdims=True))
        p = p / jnp.sum(p, axis=-1, keepdims=True)
        o_ref[:, sl] = _dot(p.astype(BF16), v_ref[0, 0, :, sl]).astype(o_ref.dtype)


def _xattn_prompt(layer, qx, mk_b, mv_b, batch, seq, tm):
    e = qx.shape[1]
    n_mem = mk_b.shape[2]
    per = seq // tm
    kv_spec = pl.BlockSpec((1, 1, n_mem, e), lambda i: (layer, i // per, 0, 0))
    return pl.pallas_call(
        _xattn_prompt_kernel,
        grid=(batch * per,),
        in_specs=[pl.BlockSpec((tm, e), lambda i: (i, 0)), kv_spec, kv_spec],
        out_specs=pl.BlockSpec((tm, e), lambda i: (i, 0)),
        out_shape=jax.ShapeDtypeStruct((batch * seq, e), BF16),
        compiler_params=_params("parallel"),
        name="xattn_prompt",
    )(qx, mk_b, mv_b)


def _xattn_sample_kernel(q_ref, k_ref, v_ref, o_ref):
    q = q_ref[...].astype(F32)
    k = k_ref[0].astype(BF16).astype(F32)
    v = v_ref[0].astype(BF16).astype(F32)
    s = jnp.sum(q * k, axis=-1, keepdims=True)
    p = jnp.exp(s - jnp.max(s, axis=1, keepdims=True))
    p = p / jnp.sum(p, axis=1, keepdims=True)
    p = p.astype(BF16).astype(F32)
    o_ref[...] = jnp.sum(p * v, axis=1, keepdims=True).astype(o_ref.dtype)


def _xattn_sample(layer, qx_s, mem_k, mem_v, nb):
    n, _, heads, dh = qx_s.shape
    n_mem = mem_k.shape[2]
    kv_spec = pl.BlockSpec((1, nb, n_mem, heads, dh), lambda i: (layer, i, 0, 0, 0))
    q_spec = pl.BlockSpec((nb, 1, heads, dh), lambda i: (i, 0, 0, 0))
    return pl.pallas_call(
        _xattn_sample_kernel,
        grid=(n // nb,),
        in_specs=[q_spec, kv_spec, kv_spec],
        out_specs=q_spec,
        out_shape=jax.ShapeDtypeStruct((n, 1, heads, dh), BF16),
        compiler_params=_params("parallel"),
        name="xattn_sample",
    )(qx_s, mem_k, mem_v)


def _route_kernel(x_ref, o_ref, wo_ref, g_ref, wr_hi_ref, wr_lo_ref, br_ref,
                  xo_ref, xn_ref, route_ref):
    x = x_ref[...] + _dot(o_ref[...], wo_ref[...])
    xo_ref[...] = x
    xn = _rms(x, g_ref[...])
    xn_ref[...] = xn
    hi = xn.astype(BF16)
    lo = (xn - hi.astype(F32)).astype(BF16)
    logits = (_dot(hi, wr_hi_ref[...]) + _dot(lo, wr_hi_ref[...]) + _dot(hi, wr_lo_ref[...])) + br_ref[...]

    lane = lax.broadcasted_iota(jnp.int32, logits.shape, 1).astype(F32)
    neg = -jnp.inf
    big = float(ROUTE_LANES)

    def top1(vals):
        vmax = jnp.max(vals, axis=-1, keepdims=True)
        idx = jnp.min(jnp.where(vals == vmax, lane, big), axis=-1, keepdims=True)
        return vmax, idx

    g_vals = jnp.where(lane < N_GROUPS, logits, neg)
    g_max, g_sel = top1(g_vals)
    g_prob = 1.0 / jnp.sum(jnp.exp(g_vals - g_max), axis=-1, keepdims=True)
    e_lo = N_GROUPS + g_sel * EXPERTS_PER_GROUP
    e_vals = jnp.where((lane >= e_lo) & (lane < e_lo + EXPERTS_PER_GROUP), logits, neg)
    v1, i1 = top1(e_vals)
    v2, i2 = top1(jnp.where(lane == i1, neg, e_vals))
    r = jnp.exp(v2 - v1)
    w1 = g_prob / (1.0 + r)
    w2 = g_prob * r / (1.0 + r)
    rec = jnp.where(lane == 0, i1 - N_GROUPS, 0.0)
    rec = jnp.where(lane == 1, i2 - N_GROUPS, rec)
    rec = jnp.where(lane == 2, w1, rec)
    rec = jnp.where(lane == 3, w2, rec)
    route_ref[...] = rec


def _route(x, o, w_o, g, wr_hi, wr_lo, b_r, tm):
    t, d = x.shape
    row = lambda w: pl.BlockSpec((tm, w), lambda i: (i, 0))
    return pl.pallas_call(
        _route_kernel,
        grid=(t // tm,),
        in_specs=[row(d), row(o.shape[1]), _const_spec(w_o.shape), _const_spec((1, d)),
                  _const_spec(wr_hi.shape), _const_spec(wr_lo.shape), _const_spec((1, ROUTE_LANES))],
        out_specs=[row(d), row(d), row(ROUTE_LANES)],
        out_shape=[jax.ShapeDtypeStruct((t, d), F32), jax.ShapeDtypeStruct((t, d), F32),
                   jax.ShapeDtypeStruct((t, ROUTE_LANES), F32)],
        compiler_params=_params("parallel"),
        name="route",
    )(x, o, w_o, g, wr_hi, wr_lo, b_r)


_MOE_ROWS = 512


def _moe_kernel(r_max, tok_ref, dst_ref, tile_e_ref, nt_ref, gate_ref, x_hbm, wg_ref, wu_ref, wd_ref,
                out_hbm, xbuf, xb_ref, ybuf, gsem, ssem):
    del tile_e_ref
    tm = xb_ref.shape[0]
    i = pl.program_id(0)
    nt = nt_ref[0]
    slot = i % 2

    def gather_row(tile, into, r):
        tok = tok_ref[tile * tm + r]
        pltpu.make_async_copy(x_hbm.at[pl.ds(tok, 1)], xbuf.at[into, pl.ds(r, 1)], gsem.at[into]).start()

    def wait_gather(into):
        pltpu.make_async_copy(x_hbm.at[pl.ds(0, tm)], xbuf.at[into], gsem.at[into]).wait()

    def scatter_row(dst, frm, r):
        pltpu.make_async_copy(ybuf.at[frm, pl.ds(r, 1)], out_hbm.at[pl.ds(dst, 1)], ssem.at[frm]).start()

    def wait_scatter(frm):
        pltpu.make_async_copy(ybuf.at[frm], out_hbm.at[pl.ds(0, tm)], ssem.at[frm]).wait()

    def rows_loop(fn):
        def body(r, carry):
            fn(r)
            return carry
        lax.fori_loop(0, tm, body, 0)

    @pl.when(i == 0)
    def _():
        ybuf[...] = jnp.zeros(ybuf.shape, F32)
        rows_loop(lambda r: gather_row(0, 0, r))
        rows_loop(lambda r: scatter_row(r_max + tm + r, 0, r))

    @pl.when(i < nt)
    def _():
        wait_gather(slot)
        xb_ref[...] = xbuf[slot].astype(BF16)
        nxt = jnp.minimum(i + 1, nt - 1)
        for r in range(tm):
            gather_row(nxt, 1 - slot, r)
        for r in range(tm):
            scatter_row(dst_ref[i * tm + r], 1 - slot, r)
        x = xb_ref[...]
        h = _silu(_dot(x, wg_ref[0])) * _dot(x, wu_ref[0])
        y = _dot((h * gate_ref[...]).astype(BF16), wd_ref[0])
        wait_scatter(slot)
        ybuf[slot] = y

    @pl.when(i == nt - 1)
    def _():
        rows_loop(lambda r: scatter_row(dst_ref[(i + 1) * tm + r], slot, r))
        wait_gather(1 - slot)
        wait_scatter(1 - slot)
        wait_scatter(slot)

    @pl.when(i >= nt)
    def _():
        @pl.when(i == nt)
        def _():
            ybuf[0] = jnp.zeros(ybuf.shape[1:], F32)

        fill = pltpu.make_async_copy(ybuf.at[0], out_hbm.at[pl.ds(pl.multiple_of(i * tm, tm), tm)], ssem.at[0])
        fill.start()
        fill.wait()


def _moe(xn, tok, dst, tile_e, n_tiles, gate_rows, w_gate, w_up, w_down):
    t, d = xn.shape
    tm = _MOE_ROWS
    r_max = tok.shape[0]
    f = w_gate.shape[2]
    dst = jnp.concatenate([r_max + jnp.arange(tm, dtype=jnp.int32), dst])

    def w_index(i, tok_r, dst_r, tile_e_r, nt_r):
        return (tile_e_r[jnp.minimum(i, nt_r[0] - 1)], 0, 0)

    grid_spec = pltpu.PrefetchScalarGridSpec(
        num_scalar_prefetch=4,
        grid=(r_max // tm,),
        in_specs=[pl.BlockSpec((tm, 1), lambda i, *_: (i, 0)),
                  pl.BlockSpec(memory_space=pl.ANY),
                  pl.BlockSpec((1, d, f), w_index), pl.BlockSpec((1, d, f), w_index),
                  pl.BlockSpec((1, f, d), w_index)],
        out_specs=pl.BlockSpec(memory_space=pl.ANY),
        scratch_shapes=[pltpu.VMEM((2, tm, d), F32), pltpu.VMEM((tm, d), BF16), pltpu.VMEM((2, tm, d), F32),
                        pltpu.SemaphoreType.DMA((2,)), pltpu.SemaphoreType.DMA((2,))])
    return pl.pallas_call(
        functools.partial(_moe_kernel, r_max),
        grid_spec=grid_spec,
        out_shape=jax.ShapeDtypeStruct((r_max + 2 * tm, d), F32),
        compiler_params=_params("arbitrary", row_dma=True),
        name="moe",
    )(tok, dst, tile_e, n_tiles, gate_rows, xn, w_gate, w_up, w_down)


def _moe_rows_max(t):
    tm = _MOE_ROWS
    return ((2 * t + N_EXPERTS * (tm - 1)) // tm) * tm


def _moe_plan(route, t):
    tm = _MOE_ROWS
    r_max = _moe_rows_max(t)
    e_flat = route[:, :2].astype(jnp.int32).reshape(-1)
    w_flat = route[:, 2:4].reshape(-1)
    onehot = (e_flat[:, None] == jnp.arange(N_EXPERTS)[None, :]).astype(F32)
    ranks = _count_prefix(onehot)
    counts = ranks[-1].astype(jnp.int32)
    rank = jnp.sum(ranks * onehot, axis=1).astype(jnp.int32) - 1
    padded = (counts + tm - 1) // tm * tm
    ends = jnp.cumsum(padded)
    pos = jnp.sum(onehot.astype(jnp.int32) * (ends - padded)[None, :], axis=1) + rank
    n_rows = ends[-1]
    pair = jnp.arange(2 * t, dtype=jnp.int32)
    row_pair = jnp.full((r_max,), -1, jnp.int32).at[pos].set(pair)
    valid = row_pair >= 0
    tok = jnp.where(valid, row_pair // 2, 0)
    n_pad_before = _count_prefix((~valid).astype(F32)[:, None])[:, 0].astype(jnp.int32) - 1
    dst = jnp.where(valid, (row_pair % 2) * t + row_pair // 2, 2 * t + n_pad_before)
    gate_rows = jnp.where(valid, w_flat[jnp.maximum(row_pair, 0)], 0.0)[:, None]
    tile_start = jnp.arange(r_max // tm, dtype=jnp.int32) * tm
    tile_e = jnp.minimum(jnp.sum((ends[None, :] <= tile_start[:, None]).astype(jnp.int32), axis=1), N_EXPERTS - 1)
    n_tiles = (n_rows // tm).astype(jnp.int32).reshape(1)
    return tok.astype(jnp.int32), dst.astype(jnp.int32), tile_e.astype(jnp.int32), n_tiles, gate_rows


def _count_prefix(flags):
    n, k = flags.shape
    blocks = flags.reshape(n // LANES, LANES, k)
    tril = jnp.tril(jnp.ones((LANES, LANES), F32))
    within = jnp.einsum("ij,bjk->bik", tril, blocks)
    totals = within[:, -1, :]
    tril_b = jnp.tril(jnp.ones((n // LANES, n // LANES), F32), -1)
    before = jnp.einsum("ab,bk->ak", tril_b, totals)
    return (within + before[:, None, :]).reshape(n, k)


def _final_kernel(x_ref, y0_ref, y1_ref, g_ref, o_ref):
    o_ref[...] = _rms(x_ref[...] + y0_ref[...] + y1_ref[...], g_ref[...])


def _final(x, ypair, g, tm):
    t, d = x.shape
    nblk = t // tm
    row = pl.BlockSpec((tm, d), lambda i: (i, 0))
    return pl.pallas_call(
        _final_kernel,
        grid=(nblk,),
        in_specs=[row, row, pl.BlockSpec((tm, d), lambda i: (i + nblk, 0)), _const_spec((1, d))],
        out_specs=row,
        out_shape=jax.ShapeDtypeStruct((t, d), F32),
        compiler_params=_params("parallel"),
        name="final_norm",
    )(x, ypair, ypair, g)


def _rope_tables(pos):
    half = MLA_ROPE // 2
    inv = ROPE_THETA ** (-jnp.arange(half, dtype=F32) / half)
    ang = pos.astype(F32)[:, None] * inv[None, :]
    cos, sin = jnp.cos(ang), jnp.sin(ang)
    pad = jnp.zeros((pos.shape[0], LANES - MLA_ROPE), F32)
    return (jnp.concatenate([cos, cos, pad], axis=1), jnp.concatenate([-sin, sin, pad], axis=1))


def _swap_halves(w):
    half = w.shape[-1] // 2
    return jnp.concatenate([w[..., half:], w[..., :half]], axis=-1)


def _pad_lanes(w, width=LANES):
    return jnp.pad(w, [(0, 0)] * (w.ndim - 1) + [(0, width - w.shape[-1])])


def _token_tile(t):
    for tm in (320, 256, 128):
        if t % tm == 0:
            return tm
    raise ValueError(f"token count {t} is not a multiple of {LANES}")


def kernel(x_prompt, x_sample, cache_mla_ckv, cache_mla_krope, cache_fox_k, cache_fox_v, cache_fox_logf, cache_mem_k, cache_mem_v, state_conv, page_table, mem_prompt, norm_mix, w_in, conv_w, conv_b, conv_ln_g, conv_ln_b, mla_q_norm, mla_w_uq, mla_kv_norm, mla_w_uk, mla_w_uv, fox_f_bias, w_out, norm_xattn, norm_mem, xa_w_q, xa_w_kv, xa_w_o, norm_ffn, router_group_w, router_group_b, router_expert_w, router_expert_b, exp_w_gate, exp_w_up, exp_w_down, norm_final):
    batch, seq, d = x_prompt.shape
    n_dec, t_dec, _ = x_sample.shape
    assert t_dec == 1, "the sample group decodes one row per sequence"
    depth = w_in.shape[0]
    n_pages, page = page_table.shape[1], cache_mla_ckv.shape[2]
    assert n_pages % _PAGES_PER_STEP == 0
    tp = batch * seq
    t = tp + n_dec
    tm = _token_tile(t)
    t_att = min(512, seq)
    n_mem = mem_prompt.shape[1]
    e_xa = XA_HEADS * XA_HEAD_DIM

    pos = jnp.concatenate([jnp.tile(jnp.arange(seq), batch), jnp.full((n_dec,), n_pages * page)])
    cos_t, sin_t = _rope_tables(pos)

    sizes = (C_CONV, C_CONV, MLA_Q_RANK, MLA_KV_RANK, MLA_ROPE, FOX_HEADS * FOX_HEAD_DIM,
             FOX_HEAD_DIM, FOX_HEAD_DIM, FOX_HEADS)
    bounds = [0]
    for sz in sizes:
        bounds.append(bounds[-1] + sz)
    wa, wg_, wcq, wckv, wkr, wfq, wfk, wfv, wff = [w_in[:, :, bounds[k]:bounds[k + 1]] for k in range(9)]
    w_in_ext = jnp.concatenate([wa, wg_, wcq, wckv, wfq, wfk, wfv, _pad_lanes(wkr),
                                _pad_lanes(_swap_halves(wkr)), _pad_lanes(wff)], axis=-1).astype(BF16)
    uq_nope = mla_w_uq[..., :MLA_NOPE].reshape(depth, MLA_Q_RANK, MLA_HEADS * MLA_NOPE)
    uq_rope = mla_w_uq[..., MLA_NOPE:]
    w_uq_ext = jnp.concatenate(
        [uq_nope, _pad_lanes(uq_rope).reshape(depth, MLA_Q_RANK, MLA_HEADS * LANES),
         _pad_lanes(_swap_halves(uq_rope)).reshape(depth, MLA_Q_RANK, MLA_HEADS * LANES)], axis=-1).astype(BF16)
    w_ukt = jnp.transpose(mla_w_uk, (0, 2, 3, 1)).astype(BF16)
    w_uv = jnp.transpose(mla_w_uv, (0, 2, 1, 3)).astype(BF16)
    f_bias_row = _pad_lanes(fox_f_bias)[:, None, :]
    w_out_b = w_out.astype(BF16)
    w_q_b = xa_w_q.astype(BF16)
    w_kv_b = xa_w_kv.astype(BF16)
    w_o_b = xa_w_o.astype(BF16)
    w_r = _pad_lanes(jnp.concatenate([router_group_w, router_expert_w], axis=-1), ROUTE_LANES)
    w_r_hi = w_r.astype(BF16)
    w_r_lo = (w_r - w_r_hi.astype(F32)).astype(BF16)
    b_r = _pad_lanes(jnp.concatenate([router_group_b, router_expert_b], axis=-1), ROUTE_LANES)[:, None, :]
    w_gate_b = exp_w_gate.astype(BF16)
    w_up_b = exp_w_up.astype(BF16)
    w_down_b = exp_w_down.astype(BF16)
    row = lambda v: v[:, None, :]
    norm_mix_r, norm_xattn_r, norm_ffn_r, norm_mem_r = row(norm_mix), row(norm_xattn), row(norm_ffn), row(norm_mem)
    q_norm_r, kv_norm_r = row(mla_q_norm), row(mla_kv_norm)
    conv_b_r, ln_g_r, ln_b_r = row(conv_b), row(conv_ln_g), row(conv_ln_b)

    mk, mv, mk_b, mv_b = _memkv(mem_prompt.reshape(batch * n_mem, d), norm_mem_r, w_kv_b, n_mem)
    mk_b = mk_b.reshape(depth, batch, n_mem, e_xa)
    mv_b = mv_b.reshape(depth, batch, n_mem, e_xa)

    x = jnp.concatenate([x_prompt.reshape(tp, d), x_sample.reshape(n_dec, d)], axis=0)
    ypair = None
    leaves = {k: [] for k in ("ckv", "kr", "fk", "fv", "logf", "conv_p", "conv_s")}
    for l in range(depth):
        (x, u, ckv, kr, fk, fv, logf, kcat, qcat, fq_b, fk_b, fv_b) = _proj(
            x, ypair, norm_mix_r[l], w_in_ext[l], w_uq_ext[l], w_ukt[l], q_norm_r[l], kv_norm_r[l],
            f_bias_row[l], cos_t, sin_t, tm)
        for name, val in (("ckv", ckv), ("kr", kr), ("fk", fk), ("fv", fv), ("logf", logf)):
            leaves[name].append(val)
        u_p = u[:tp].reshape(batch, seq, C_CONV)
        u_s = u[tp:]
        leaves["conv_p"].append(u_p[:, seq - (CONV_WIDTH - 1):])
        leaves["conv_s"].append(jnp.concatenate([state_conv[l][:, 1:], u_s[:, None, :]], axis=1))
        conv_p = _conv_prompt(u, batch, seq, conv_w[l], conv_b_r[l], ln_g_r[l], ln_b_r[l], min(256, seq))
        conv_s = _conv_sample(state_conv[l], u_s, conv_w[l], conv_b_r[l], ln_g_r[l], ln_b_r[l], min(32, n_dec))
        lat_p = _flash(qcat, kcat, kcat, None, None, batch, seq, MLA_HEADS, MLA_QK, MLA_KV_RANK, t_att)
        lf_t = jnp.pad(jnp.transpose(logf[:tp].reshape(batch, seq, FOX_HEADS), (0, 2, 1)),
                       ((0, 0), (0, HEADS_PAD - FOX_HEADS), (0, 0)))
        cum_row = _cumsum_time(lf_t)
        cum_col = jnp.transpose(cum_row, (0, 2, 1)).reshape(tp, HEADS_PAD)
        fox_p = _flash(fq_b, fk_b, fv_b, cum_col, cum_row, batch, seq, FOX_HEADS, FOX_HEAD_DIM, FOX_HEAD_DIM, t_att)
        pad_heads = lambda a: jnp.pad(a, ((0, 0), (0, HEADS_PAD - a.shape[1]), (0, 0)))
        q_s = pad_heads(qcat[tp:].reshape(n_dec, MLA_HEADS, MLA_QK))
        fq_s = pad_heads(fq_b[tp:].reshape(n_dec, FOX_HEADS, FOX_HEAD_DIM))
        cn = pad_heads(logf[tp:][:, :, None])
        lat_s, fox_s = _decode(l, page_table, q_s, fq_s, kcat[tp:][:, None, :], fk_b[tp:][:, None, :],
                               fv_b[tp:][:, None, :], cn, cache_mla_ckv, cache_mla_krope, cache_fox_k,
                               cache_fox_v, cache_fox_logf)
        lat_s = lat_s[:, :MLA_HEADS].reshape(n_dec, MLA_HEADS * MLA_KV_RANK)
        fox_s = fox_s[:, :FOX_HEADS].reshape(n_dec, FOX_HEADS * FOX_HEAD_DIM)
        x, qx = _merge(x, jnp.concatenate([conv_p, conv_s]), jnp.concatenate([lat_p, lat_s]),
                       jnp.concatenate([fox_p, fox_s]), w_uv[l], w_out_b[l], norm_xattn_r[l], w_q_b[l], tm)
        o_p = _xattn_prompt(l, qx, mk_b, mv_b, batch, seq, min(512, seq))
        o_s = _xattn_sample(l, qx[tp:].reshape(n_dec, 1, XA_HEADS, XA_HEAD_DIM), cache_mem_k, cache_mem_v,
                            min(4, n_dec))
        x, xn, route = _route(x, jnp.concatenate([o_p, o_s.reshape(n_dec, e_xa)]), w_o_b[l], norm_ffn_r[l],
                              w_r_hi[l], w_r_lo[l], b_r[l], tm)
        tok, dst, tile_e, n_tiles, gate_rows = _moe_plan(route, t)
        ypair = _moe(xn, tok, dst, tile_e, n_tiles, gate_rows, w_gate_b[l], w_up_b[l], w_down_b[l])
    y = _final(x, ypair, norm_final[None, :], tm)

    def stack_p(name, width):
        return jnp.stack([v[:tp].reshape(batch, seq, width) for v in leaves[name]])

    def stack_s(name, width):
        return jnp.stack([v[tp:].reshape(n_dec, 1, width) for v in leaves[name]])

    mem_shape = (depth, batch, n_mem, XA_HEADS, XA_HEAD_DIM)
    return (y[:tp].reshape(batch, seq, d), y[tp:].reshape(n_dec, 1, d),
            stack_p("ckv", MLA_KV_RANK), stack_p("kr", MLA_ROPE), stack_p("fk", FOX_HEAD_DIM),
            stack_p("fv", FOX_HEAD_DIM), stack_p("logf", FOX_HEADS),
            mk.reshape(mem_shape), mv.reshape(mem_shape), jnp.stack(leaves["conv_p"]),
            stack_s("ckv", MLA_KV_RANK), stack_s("kr", MLA_ROPE), stack_s("fk", FOX_HEAD_DIM),
            stack_s("fv", FOX_HEAD_DIM), stack_s("logf", FOX_HEADS), jnp.stack(leaves["conv_s"]))
```
